```python
import math
import jax
import jax.numpy as jnp
from jax import lax
import numpy as np

D_MODEL = 1024
BATCH = 8
SEQ = 4096
DEPTH = 4

GRID_W = 64
CTX_LEN = 256
N_EVEN = (DEPTH + 1) // 2
N_ODD = DEPTH // 2
EPS = 1e-6
NEG_INF = -1e30

MLA_HEADS = D_MODEL // 128
MLA_NOPE = 64
MLA_ROPE = 32
MLA_V = 64
MLA_QK = MLA_NOPE + MLA_ROPE
MLA_Q_RANK = 256
MLA_KV_RANK = 128
MLA_WIDTH = MLA_HEADS * MLA_V
MLA_SCALE = 1.0 / math.sqrt(MLA_QK)
ROPE_BASE = 10000.0
Q_BLOCK = 128

S5_CH = 16
S5_GROUPS = D_MODEL // 32
S5_WIDTH = S5_GROUPS * S5_CH
S5_STATE = 64
S5_DT_MIN = 0.001
S5_DT_MAX = 0.1

EV_S1 = MLA_Q_RANK
EV_S2 = EV_S1 + MLA_KV_RANK
EV_S3 = EV_S2 + MLA_ROPE
EV_S4 = EV_S3 + MLA_WIDTH
EV_S5 = EV_S4 + S5_WIDTH
EVEN_IN = EV_S5 + S5_WIDTH

NA_HEADS = D_MODEL // 64
NA_DH = 64
NA_WIDTH = NA_HEADS * NA_DH
NA_KH = 8
NA_KW = 16
NA_SCALE = 1.0 / math.sqrt(NA_DH)
ODD_IN = 4 * NA_WIDTH

kernel_name = "hybrid_mla_s5_natten_ctxprefix"


def rmsnorm(x, g):
    xf = x.astype(jnp.float32)
    y = xf * lax.rsqrt(jnp.mean(xf * xf, axis=-1, keepdims=True) + EPS)
    return (y * g.astype(jnp.float32)).astype(x.dtype)


def softmax_f32(s):
    return jax.nn.softmax(s.astype(jnp.float32), axis=-1)


def axial_rope_tables(n_tok):
    t = jnp.arange(n_tok, dtype=jnp.int32)
    row = (t // GRID_W).astype(jnp.float32)
    col = (t % GRID_W).astype(jnp.float32)
    nf = MLA_ROPE // 4
    freqs = ROPE_BASE ** (-jnp.arange(nf, dtype=jnp.float32) / nf)
    ar = row[:, None] * freqs
    ac = col[:, None] * freqs
    cos = jnp.concatenate([jnp.cos(ar), jnp.cos(ar), jnp.cos(ac), jnp.cos(ac)], axis=-1)
    sin = jnp.concatenate([jnp.sin(ar), jnp.sin(ar), jnp.sin(ac), jnp.sin(ac)], axis=-1)
    return cos, sin


def apply_axial_rope(x, cos, sin):
    x4 = x.reshape(x.shape[:-1] + (2, 2, MLA_ROPE // 4))
    rot = jnp.stack([-x4[..., 1, :], x4[..., 0, :]], axis=-2).reshape(x.shape)
    return (x * cos + rot * sin).astype(x.dtype)


def mla_project(z, q_norm, kv_norm, w_uq, w_ukv):
    B, L, _ = z.shape
    q = (rmsnorm(z[..., :EV_S1], q_norm) @ w_uq).reshape(B, L, MLA_HEADS, MLA_QK)
    kv = (rmsnorm(z[..., EV_S1:EV_S2], kv_norm) @ w_ukv).reshape(B, L, MLA_HEADS, MLA_NOPE + MLA_V)
    k_rope = z[..., EV_S2:EV_S3]
    return q[..., :MLA_NOPE], q[..., MLA_NOPE:], kv[..., :MLA_NOPE], k_rope, kv[..., MLA_NOPE:]


def mla_scores(qn, qr, kn, kr):
    return (jnp.einsum('bqhd,bkhd->bhqk', qn, kn) + jnp.einsum('bqhr,bkr->bhqk', qr, kr)) * MLA_SCALE


def mla_latent_attention(qn, qr, kn_all, kr_all, v_all):
    B, L, H, _ = qn.shape
    nblk = L // Q_BLOCK
    qn_b = qn.reshape(B, nblk, Q_BLOCK, H, MLA_NOPE).transpose(1, 0, 2, 3, 4)
    qr_b = qr.reshape(B, nblk, Q_BLOCK, H, MLA_ROPE).transpose(1, 0, 2, 3, 4)

    def step(args):
        qn_i, qr_i = args
        p = softmax_f32(mla_scores(qn_i, qr_i, kn_all, kr_all)).astype(v_all.dtype)
        return jnp.einsum('bhqk,bkhd->bqhd', p, v_all)

    o = lax.map(step, (qn_b, qr_b))
    return o.transpose(1, 0, 2, 3, 4).reshape(B, L, MLA_WIDTH)


def s5_discretise(lam_re, lam_im, log_dt, b_re, b_im):
    dt = jnp.exp(log_dt)[:, None]
    lr = jnp.minimum(lam_re, -1e-4)
    li = lam_im
    mag = jnp.exp(lr * dt)
    lbr = mag * jnp.cos(li * dt)
    lbi = mag * jnp.sin(li * dt)
    den = lr * lr + li * li
    nr = lbr - 1.0
    k_re = (nr * lr + lbi * li) / den
    k_im = (lbi * lr - nr * li) / den
    bb_re = k_re[..., None] * b_re - k_im[..., None] * b_im
    bb_im = k_re[..., None] * b_im + k_im[..., None] * b_re
    return lbr, lbi, bb_re, bb_im


def _complex_affine_combine(e1, e2):
    a1r, a1i, b1r, b1i = e1
    a2r, a2i, b2r, b2i = e2
    return (a1r * a2r - a1i * a2i,
            a1r * a2i + a1i * a2r,
            a2r * b1r - a2i * b1i + b2r,
            a2r * b1i + a2i * b1r + b2i)


def s5_scan(u, lbr, lbi, bb_re, bb_im, h0):
    bu_re = jnp.einsum('blgh,gph->blgp', u, bb_re)
    bu_im = jnp.einsum('blgh,gph->blgp', u, bb_im)
    if h0 is not None:
        h0r, h0i = h0
        bu_re = bu_re.at[:, 0].add(lbr * h0r - lbi * h0i)
        bu_im = bu_im.at[:, 0].add(lbr * h0i + lbi * h0r)
    L = u.shape[1]
    a_re = jnp.broadcast_to(lbr[None, None], (1, L) + lbr.shape)
    a_im = jnp.broadcast_to(lbi[None, None], (1, L) + lbi.shape)
    _, _, x_re, x_im = lax.associative_scan(_complex_affine_combine, (a_re, a_im, bu_re, bu_im), axis=1)
    return x_re, x_im


def s5_readout(x_re, x_im, c_re, c_im):
    return jnp.einsum('blgp,ghp->blgh', x_re, c_re) - jnp.einsum('blgp,ghp->blgh', x_im, c_im)


def s5_bidirectional(u_lat, u_ctx, need_ctx, lam_re, lam_im, log_dt, b_re, b_im, c_re, c_im, d_skip):
    f32 = jnp.float32
    B, L, _ = u_lat.shape
    Lc = u_ctx.shape[1]
    ul = u_lat.astype(f32).reshape(B, L, S5_GROUPS, S5_CH)
    uc = u_ctx.astype(f32).reshape(B, Lc, S5_GROUPS, S5_CH)
    d = d_skip.astype(f32).reshape(S5_GROUPS, S5_CH)
    y_lat = ul * d
    y_ctx = uc * d
    for di in range(2):
        rev = di == 1
        lbr, lbi, bbr, bbi = s5_discretise(lam_re[di].astype(f32), lam_im[di].astype(f32),
                                           log_dt[di].astype(f32), b_re[di].astype(f32), b_im[di].astype(f32))
        ucd = jnp.flip(uc, axis=1) if rev else uc
        uld = jnp.flip(ul, axis=1) if rev else ul
        xc_re, xc_im = s5_scan(ucd, lbr, lbi, bbr, bbi, None)
        xl_re, xl_im = s5_scan(uld, lbr, lbi, bbr, bbi, (xc_re[:, -1], xc_im[:, -1]))
        cr = c_re[di].astype(f32)
        ci = c_im[di].astype(f32)
        yl = s5_readout(xl_re, xl_im, cr, ci)
        y_lat = y_lat + (jnp.flip(yl, axis=1) if rev else yl)
        if need_ctx:
            yc = s5_readout(xc_re, xc_im, cr, ci)
            y_ctx = y_ctx + (jnp.flip(yc, axis=1) if rev else yc)
    y_lat = y_lat.reshape(B, L, S5_WIDTH).astype(u_lat.dtype)
    y_ctx = y_ctx.reshape(B, Lc, S5_WIDTH).astype(u_ctx.dtype) if need_ctx else None
    return y_lat, y_ctx


def s5_glu(y, w_glu, b_glu):
    a = jax.nn.gelu(y) @ w_glu + b_glu
    return a[..., :S5_WIDTH] * jax.nn.sigmoid(a[..., S5_WIDTH:])


def even_mixer(h, hc, need_ctx, cos, sin, w_in, q_norm, kv_norm, w_uq, w_ukv,
               lam_re, lam_im, log_dt, b_re, b_im, c_re, c_im, d_skip, w_glu, b_glu, w_out):
    B, L, _ = h.shape
    Lc = hc.shape[1]
    z = h @ w_in
    zc = hc @ w_in
    qn, qr, kn, kr, v = mla_project(z, q_norm, kv_norm, w_uq, w_ukv)
    qn_c, qr_c, kn_c, kr_c, v_c = mla_project(zc, q_norm, kv_norm, w_uq, w_ukv)
    qr = apply_axial_rope(qr, cos[None, :, None, :], sin[None, :, None, :])
    kr = apply_axial_rope(kr, cos[None], sin[None])
    kn_all = jnp.concatenate([kn_c, kn], axis=1)
    kr_all = jnp.concatenate([kr_c, kr], axis=1)
    v_all = jnp.concatenate([v_c, v], axis=1)
    o_mla = mla_latent_attention(qn, qr, kn_all, kr_all, v_all)
    y_s5, y_s5_c = s5_bidirectional(z[..., EV_S4:EV_S5], zc[..., EV_S4:EV_S5], need_ctx,
                                    lam_re, lam_im, log_dt, b_re, b_im, c_re, c_im, d_skip)
    branch = jnp.concatenate([o_mla * jax.nn.silu(z[..., EV_S3:EV_S4]),
                              s5_glu(y_s5, w_glu, b_glu) * jax.nn.silu(z[..., EV_S5:])], axis=-1)
    out = branch @ w_out
    if not need_ctx:
        return out, None
    p_c = softmax_f32(mla_scores(qn_c, qr_c, kn_c, kr_c)).astype(v_c.dtype)
    o_c = jnp.einsum('bhqk,bkhd->bqhd', p_c, v_c).reshape(B, Lc, MLA_WIDTH)
    branch_c = jnp.concatenate([o_c * jax.nn.silu(zc[..., EV_S3:EV_S4]),
                                s5_glu(y_s5_c, w_glu, b_glu) * jax.nn.silu(zc[..., EV_S5:])], axis=-1)
    return out, branch_c @ w_out


def odd_mixer(h, hc, need_ctx, w_in, rpb, w_out):
    B, L, _ = h.shape
    Lc = hc.shape[1]
    rows = L // GRID_W
    kh = min(NA_KH, rows)
    nk = kh * GRID_W
    z = (h @ w_in).reshape(B, L, 4, NA_HEADS, NA_DH)
    q, k, v = z[:, :, 0], z[:, :, 1], z[:, :, 2]
    gate = z[:, :, 3].reshape(B, L, NA_WIDTH)
    zc = (hc @ w_in).reshape(B, Lc, 4, NA_HEADS, NA_DH)
    qc, kc, vc = zc[:, :, 0], zc[:, :, 1], zc[:, :, 2]
    gate_c = zc[:, :, 3].reshape(B, Lc, NA_WIDTH)

    qg = q.reshape(B, rows, GRID_W, NA_HEADS, NA_DH).transpose(1, 0, 2, 3, 4)
    kg = k.reshape(B, rows, GRID_W, NA_HEADS, NA_DH)
    vg = v.reshape(B, rows, GRID_W, NA_HEADS, NA_DH)
    col = jnp.arange(GRID_W, dtype=jnp.int32)
    cs = jnp.clip(col - NA_KW // 2, 0, GRID_W - NA_KW)
    col_mask = (col[None, :] >= cs[:, None]) & (col[None, :] < cs[:, None] + NA_KW)
    win_mask = jnp.tile(col_mask, (1, kh))
    dc_idx = jnp.clip(col[None, :] - col[:, None] + NA_KW - 1, 0, 2 * NA_KW - 2)

    def row_step(args):
        q_row, r = args
        rs = jnp.clip(r - kh // 2, 0, rows - kh)
        k_blk = lax.dynamic_slice_in_dim(kg, rs, kh, axis=1).reshape(B, nk, NA_HEADS, NA_DH)
        v_blk = lax.dynamic_slice_in_dim(vg, rs, kh, axis=1).reshape(B, nk, NA_HEADS, NA_DH)
        dr_idx = rs + jnp.arange(kh, dtype=jnp.int32) - r + NA_KH - 1
        bias = rpb[:, dr_idx][:, :, dc_idx]
        bias = bias.transpose(0, 2, 1, 3).reshape(NA_HEADS, GRID_W, nk)
        s_win = jnp.einsum('bqhd,bkhd->bhqk', q_row, k_blk) * NA_SCALE + bias
        s_win = jnp.where(win_mask, s_win.astype(jnp.float32), NEG_INF)
        s_ctx = (jnp.einsum('bqhd,bkhd->bhqk', q_row, kc) * NA_SCALE).astype(jnp.float32)
        p = softmax_f32(jnp.concatenate([s_win, s_ctx], axis=-1)).astype(v_blk.dtype)
        return (jnp.einsum('bhqk,bkhd->bqhd', p[..., :nk], v_blk)
                + jnp.einsum('bhqk,bkhd->bqhd', p[..., nk:], vc))

    o = lax.map(row_step, (qg, jnp.arange(rows, dtype=jnp.int32)))
    o = o.transpose(1, 0, 2, 3, 4).reshape(B, L, NA_WIDTH)
    out = (o * jax.nn.silu(gate)) @ w_out
    if not need_ctx:
        return out, None
    p_c = softmax_f32(jnp.einsum('bqhd,bkhd->bhqk', qc, kc) * NA_SCALE).astype(vc.dtype)
    o_c = jnp.einsum('bhqk,bkhd->bqhd', p_c, vc).reshape(B, Lc, NA_WIDTH)
    return out, (o_c * jax.nn.silu(gate_c)) @ w_out


def setup_inputs(seed: int = 0) -> dict:
    key = jax.random.key(seed)
    ks = jax.random.split(key, 32)
    f32 = jnp.float32

    def nrm(k, shape, s):
        return jax.random.normal(k, shape, f32) * s

    D = D_MODEL
    lam_im_base = jnp.pi * jnp.arange(S5_STATE, dtype=f32)
    return {
        "x": nrm(ks[0], (BATCH, SEQ, D), 1.0),
        "c": nrm(ks[1], (BATCH, D), 1.0),
        "ctx": nrm(ks[2], (BATCH, CTX_LEN, D), 1.0),
        "c_ctx": nrm(ks[3], (D,), 1.0),
        "ada_w": nrm(ks[4], (DEPTH, D, 3 * D), D ** -0.5),
        "ada_b": nrm(ks[5], (DEPTH, 3 * D), 0.02),
        "norm_g": 1.0 + nrm(ks[6], (DEPTH, D), 0.05),
        "final_g": 1.0 + nrm(ks[7], (D,), 0.05),
        "ev_w_in": nrm(ks[8], (N_EVEN, D, EVEN_IN), D ** -0.5),
        "ev_q_norm": 1.0 + nrm(ks[9], (N_EVEN, MLA_Q_RANK), 0.05),
        "ev_kv_norm": 1.0 + nrm(ks[10], (N_EVEN, MLA_KV_RANK), 0.05),
        "ev_w_uq": nrm(ks[11], (N_EVEN, MLA_Q_RANK, MLA_HEADS * MLA_QK), MLA_Q_RANK ** -0.5),
        "ev_w_ukv": nrm(ks[12], (N_EVEN, MLA_KV_RANK, MLA_HEADS * (MLA_NOPE + MLA_V)), MLA_KV_RANK ** -0.5),
        "s5_lam_re": -0.5 + nrm(ks[13], (N_EVEN, 2, S5_GROUPS, S5_STATE), 0.01),
        "s5_lam_im": lam_im_base + nrm(ks[14], (N_EVEN, 2, S5_GROUPS, S5_STATE), 0.01),
        "s5_log_dt": jax.random.uniform(ks[15], (N_EVEN, 2, S5_GROUPS), f32,
                                        minval=math.log(S5_DT_MIN), maxval=math.log(S5_DT_MAX)),
        "s5_b_re": nrm(ks[16], (N_EVEN, 2, S5_GROUPS, S5_STATE, S5_CH), (2 * S5_CH) ** -0.5),
        "s5_b_im": nrm(ks[17], (N_EVEN, 2, S5_GROUPS, S5_STATE, S5_CH), (2 * S5_CH) ** -0.5),
        "s5_c_re": nrm(ks[18], (N_EVEN, 2, S5_GROUPS, S5_CH, S5_STATE), 0.5),
        "s5_c_im": nrm(ks[19], (N_EVEN, 2, S5_GROUPS, S5_CH, S5_STATE), 0.5),
        "s5_d": nrm(ks[20], (N_EVEN, S5_WIDTH), 1.0),
        "s5_w_glu": nrm(ks[21], (N_EVEN, S5_WIDTH, 2 * S5_WIDTH), S5_WIDTH ** -0.5),
        "s5_b_glu": nrm(ks[22], (N_EVEN, 2 * S5_WIDTH), 0.02),
        "ev_w_out": nrm(ks[23], (N_EVEN, MLA_WIDTH + S5_WIDTH, D), (MLA_WIDTH + S5_WIDTH) ** -0.5),
        "na_w_in": nrm(ks[24], (N_ODD, D, ODD_IN), D ** -0.5),
        "na_rpb": nrm(ks[25], (N_ODD, NA_HEADS, 2 * NA_KH - 1, 2 * NA_KW - 1), 0.1),
        "na_w_out": nrm(ks[26], (N_ODD, NA_WIDTH, D), NA_WIDTH ** -0.5),
    }


def reference(x, c, ctx, c_ctx, ada_w, ada_b, norm_g, final_g,
              ev_w_in, ev_q_norm, ev_kv_norm, ev_w_uq, ev_w_ukv,
              s5_lam_re, s5_lam_im, s5_log_dt, s5_b_re, s5_b_im, s5_c_re, s5_c_im, s5_d,
              s5_w_glu, s5_b_glu, ev_w_out,
              na_w_in, na_rpb, na_w_out):
    n_tok = x.shape[1]
    cos, sin = axial_rope_tables(n_tok)
    sc = jax.nn.silu(c)
    scc = jax.nn.silu(c_ctx)
    for l in range(DEPTH):
        need_ctx = l < DEPTH - 1
        mod = sc @ ada_w[l] + ada_b[l]
        mod_c = scc @ ada_w[l] + ada_b[l]
        sh, scl, gt = jnp.split(mod, 3, axis=-1)
        sh_c, scl_c, gt_c = jnp.split(mod_c, 3, axis=-1)
        h = rmsnorm(x, norm_g[l]) * (1.0 + scl[:, None, :]) + sh[:, None, :]
        hc = rmsnorm(ctx, norm_g[l]) * (1.0 + scl_c) + sh_c
        i = l // 2
        if l % 2 == 0:
            out, out_c = even_mixer(h, hc, need_ctx, cos, sin, ev_w_in[i], ev_q_norm[i], ev_kv_norm[i],
                                    ev_w_uq[i], ev_w_ukv[i], s5_lam_re[i], s5_lam_im[i], s5_log_dt[i],
                                    s5_b_re[i], s5_b_im[i], s5_c_re[i], s5_c_im[i], s5_d[i],
                                    s5_w_glu[i], s5_b_glu[i], ev_w_out[i])
        else:
            out, out_c = odd_mixer(h, hc, need_ctx, na_w_in[i], na_rpb[i], na_w_out[i])
        x = x + gt[:, None, :] * out
        if need_ctx:
            ctx = ctx + gt_c * out_c
    return rmsnorm(x, final_g)
```

```python
import functools
import math

import jax
import jax.numpy as jnp
from jax import lax
from jax.experimental import pallas as pl
from jax.experimental.pallas import tpu as pltpu

F32 = jnp.float32
BF16 = jnp.bfloat16

SUBLANES = 8
LANES = 128
VMEM_LIMIT_BYTES = 48 * 1024 * 1024

GRID_W = 64
EPS = 1e-6
NEG_INF = -1e30
MLA_HEADS = 8
MLA_NOPE = 64
MLA_ROPE = 32
MLA_V = 64
MLA_QK = MLA_NOPE + MLA_ROPE
MLA_Q_RANK = 256
MLA_KV_RANK = 128
MLA_WIDTH = MLA_HEADS * MLA_V
MLA_SCALE = 1.0 / math.sqrt(MLA_QK)
ROPE_BASE = 10000.0
S5_CH = 16
S5_GROUPS = 32
S5_WIDTH = S5_GROUPS * S5_CH
S5_STATE = 64
S5_HALVES = 2
S5_HALF_IN = S5_WIDTH // S5_HALVES
S5_HALF_STATES = S5_GROUPS * S5_STATE // S5_HALVES
S5_POWERS = SUBLANES
NA_HEADS = 16
NA_DH = 64
NA_WIDTH = NA_HEADS * NA_DH
NA_KH = 8
NA_KW = 16
NA_SCALE = 1.0 / math.sqrt(NA_DH)

TOKEN_BLOCK = 256
NA_Q_ROWS = TOKEN_BLOCK // GRID_W
NA_WIN_BLOCKS = 3
EVEN_SMALL = MLA_Q_RANK + MLA_KV_RANK + 2 * LANES


def _cparams(sem):
    return pltpu.CompilerParams(dimension_semantics=sem, vmem_limit_bytes=VMEM_LIMIT_BYTES)


def _dot(a, b):
    return jnp.dot(a, b, preferred_element_type=F32)


def _dot_nt(a, b):
    return lax.dot_general(a, b, (((1,), (1,)), ((), ())), preferred_element_type=F32)


def _rms(x, g):
    return x * lax.rsqrt(jnp.mean(x * x, axis=-1, keepdims=True) + EPS) * g


def _modnorm(x, g, mod):
    return _rms(x, g) * (1.0 + mod[1:2]) + mod[0:1]


def _adaln_kernel(c_ref, w_ref, b_ref, o_ref):
    sc = jax.nn.silu(c_ref[...]).astype(BF16)
    o_ref[0] = _dot(sc, w_ref[0].astype(BF16)) + b_ref[0]


def _adaln(c, c_ctx, ada_w, ada_b):
    depth, d, d3 = ada_w.shape
    b = c.shape[0]
    rows = -(-(b + 1) // SUBLANES) * SUBLANES
    call = jnp.concatenate([c, c_ctx[None], jnp.zeros((rows - b - 1, d), F32)], axis=0)
    tn = d
    out = pl.pallas_call(
        _adaln_kernel,
        grid=(depth, d3 // tn),
        in_specs=[pl.BlockSpec((rows, d), lambda l, n: (0, 0)),
                  pl.BlockSpec((1, d, tn), lambda l, n: (l, 0, n)),
                  pl.BlockSpec((1, 1, tn), lambda l, n: (l, 0, n))],
        out_specs=pl.BlockSpec((1, rows, tn), lambda l, n: (l, 0, n)),
        out_shape=jax.ShapeDtypeStruct((depth, rows, d3), F32),
        compiler_params=_cparams(("arbitrary", "arbitrary")),
        name="adaln",
    )(call, ada_w, ada_b[:, None, :])
    lat = out[:, :b].reshape(depth, b, 1, 3, d)
    cx = jnp.broadcast_to(out[:, b].reshape(depth, 1, 1, 3, d), (depth, b, 1, 3, d))
    return jnp.concatenate([cx, lat], axis=2)


def _seg(t):
    return jnp.minimum(t, 1)


def _even_in_kernel(x_ref, g_ref, mod_ref, cos_ref, sin_ref, win_ref, qn_ref, kvn_ref,
                    wuqa_ref, wuqb_ref, wukv_ref,
                    q_ref, k_ref, v_ref, gm_ref, u_ref, gs_ref):
    h = _modnorm(x_ref[0], g_ref[...], mod_ref[0, 0]).astype(BF16)
    z = _dot(h, win_ref[...])
    o = EVEN_SMALL
    gm_ref[0] = z[:, o:o + MLA_WIDTH]
    u_ref[0] = z[:, o + MLA_WIDTH:o + MLA_WIDTH + S5_WIDTH]
    gs_ref[0] = z[:, o + MLA_WIDTH + S5_WIDTH:]
    cos = cos_ref[...]
    sin = sin_ref[...]
    nq = _rms(z[:, :MLA_Q_RANK], qn_ref[...]).astype(BF16)
    qa = _dot(nq, wuqa_ref[...])
    qb = _dot(nq, wuqb_ref[...])
    nkv = _rms(z[:, MLA_Q_RANK:MLA_Q_RANK + MLA_KV_RANK], kvn_ref[...]).astype(BF16)
    kv = _dot(nkv, wukv_ref[...])
    s0 = MLA_Q_RANK + MLA_KV_RANK
    kr = z[:, s0:s0 + LANES] * cos + z[:, s0 + LANES:s0 + 2 * LANES] * sin
    for hd in range(MLA_HEADS):
        sl = slice(hd * LANES, (hd + 1) * LANES)
        q_ref[0, :, sl] = ((qa[:, sl] * cos + qb[:, sl] * sin) * MLA_SCALE).astype(BF16)
        k_ref[0, :, sl] = (kv[:, sl] + kr).astype(BF16)
    v_ref[0] = kv[:, MLA_HEADS * LANES:].astype(BF16)


def _even_in(xc, g, mod, cos_t, sin_t, w):
    b, s, d = xc.shape
    tb = TOKEN_BLOCK
    hs = MLA_HEADS * LANES
    nin = w["win"].shape[1]
    const = lambda bb, t: (0, 0)
    tok = lambda width: pl.BlockSpec((1, tb, width), lambda bb, t: (bb, t, 0))
    return pl.pallas_call(
        _even_in_kernel,
        grid=(b, s // tb),
        in_specs=[tok(d),
                  pl.BlockSpec((1, d), const),
                  pl.BlockSpec((1, 1, 3, d), lambda bb, t: (bb, _seg(t), 0, 0)),
                  pl.BlockSpec((tb, LANES), lambda bb, t: (t, 0)),
                  pl.BlockSpec((tb, LANES), lambda bb, t: (t, 0)),
                  pl.BlockSpec((d, nin), const),
                  pl.BlockSpec((1, MLA_Q_RANK), const),
                  pl.BlockSpec((1, MLA_KV_RANK), const),
                  pl.BlockSpec((MLA_Q_RANK, hs), const),
                  pl.BlockSpec((MLA_Q_RANK, hs), const),
                  pl.BlockSpec((MLA_KV_RANK, hs + MLA_WIDTH), const)],
        out_specs=[tok(hs), tok(hs), tok(MLA_WIDTH), tok(MLA_WIDTH), tok(S5_WIDTH), tok(S5_WIDTH)],
        out_shape=[jax.ShapeDtypeStruct((b, s, hs), BF16),
                   jax.ShapeDtypeStruct((b, s, hs), BF16),
                   jax.ShapeDtypeStruct((b, s, MLA_WIDTH), BF16),
                   jax.ShapeDtypeStruct((b, s, MLA_WIDTH), F32),
                   jax.ShapeDtypeStruct((b, s, S5_WIDTH), F32),
                   jax.ShapeDtypeStruct((b, s, S5_WIDTH), F32)],
        compiler_params=_cparams(("parallel", "parallel")),
        name="even_in",
    )(xc, g, mod, cos_t, sin_t, w["win"], w["qn"], w["kvn"], w["wuqa"], w["wuqb"], w["wukv"])


def _mla_kernel(q_ref, k_ref, v_ref, o_ref, *, n_ctx, n_all):
    lane = lax.broadcasted_iota(jnp.int32, (TOKEN_BLOCK, LANES), 1)

    def attend(nk):
        outs = []
        for hh in range(2):
            sl = slice(hh * LANES, (hh + 1) * LANES)
            s = _dot_nt(q_ref[0, :, sl], k_ref[0, 0:nk, sl])
            p = jnp.exp(s - jnp.max(s, axis=-1, keepdims=True))
            l = jnp.sum(p, axis=-1, keepdims=True)
            outs.append(_dot(p.astype(BF16), v_ref[0, 0:nk, :]) / l)
        o_ref[0] = jnp.where(lane < MLA_V, outs[0], outs[1])

    @pl.when(pl.program_id(2) == 0)
    def _():
        attend(n_ctx)

    @pl.when(pl.program_id(2) > 0)
    def _():
        attend(n_all)


def _mla(q, k, v):
    b, s, _ = q.shape
    tb = TOKEN_BLOCK
    return pl.pallas_call(
        functools.partial(_mla_kernel, n_ctx=tb, n_all=s),
        grid=(b, MLA_HEADS // 2, s // tb),
        in_specs=[pl.BlockSpec((1, tb, 2 * LANES), lambda bb, hp, t: (bb, t, hp)),
                  pl.BlockSpec((1, s, 2 * LANES), lambda bb, hp, t: (bb, 0, hp)),
                  pl.BlockSpec((1, s, LANES), lambda bb, hp, t: (bb, 0, hp))],
        out_specs=pl.BlockSpec((1, tb, LANES), lambda bb, hp, t: (bb, t, hp)),
        out_shape=jax.ShapeDtypeStruct((b, s, MLA_WIDTH), F32),
        compiler_params=_cparams(("parallel", "parallel", "arbitrary")),
        name="mla_attn",
    )(q, k, v)


def _s5_prep_kernel(lre_ref, lim_ref, ldt_ref, bre_ref, bim_ref,
                    bbre_ref, bbim_ref, pwre_ref, pwim_ref):
    dt = jnp.exp(ldt_ref[0])
    lr = jnp.minimum(lre_ref[0], -1e-4)
    li = lim_ref[0]
    mag = jnp.exp(lr * dt)
    lbr = mag * jnp.cos(li * dt)
    lbi = mag * jnp.sin(li * dt)
    den = lr * lr + li * li
    nr = lbr - 1.0
    k_re = (nr * lr + lbi * li) / den
    k_im = (lbi * lr - nr * li) / den
    b_re = bre_ref[0]
    b_im = bim_ref[0]
    bbre_ref[0] = k_re * b_re - k_im * b_im
    bbim_ref[0] = k_re * b_im + k_im * b_re
    pr, pi = lbr, lbi
    for j in range(S5_POWERS):
        pwre_ref[0, j:j + 1, :] = pr
        pwim_ref[0, j:j + 1, :] = pi
        pr, pi = pr * lbr - pi * lbi, pr * lbi + pi * lbr


def _blockdiag(m):
    n, r, c = m.shape
    eye = jnp.eye(n, dtype=m.dtype)
    return (m[:, :, None, :] * eye[:, None, :, None]).reshape(n * r, n * c)


def _s5_prepare(lam_re, lam_im, log_dt, b_re, b_im, c_re, c_im):
    ndir, g, p = lam_re.shape
    ch = b_re.shape[-1]
    gp = g * p
    flat = lambda a: a.reshape(ndir, 1, gp)
    ldt = flat(jnp.broadcast_to(log_dt[:, :, None], (ndir, g, p)))
    tr = lambda a: a.transpose(0, 3, 1, 2).reshape(ndir, ch, gp)
    row = pl.BlockSpec((1, 1, gp), lambda d: (d, 0, 0))
    mat = pl.BlockSpec((1, ch, gp), lambda d: (d, 0, 0))
    pw = pl.BlockSpec((1, S5_POWERS, gp), lambda d: (d, 0, 0))
    bb_re, bb_im, pw_re, pw_im = pl.pallas_call(
        _s5_prep_kernel,
        grid=(ndir,),
        in_specs=[row, row, row, mat, mat],
        out_specs=[mat, mat, pw, pw],
        out_shape=[jax.ShapeDtypeStruct((ndir, ch, gp), F32)] * 2
        + [jax.ShapeDtypeStruct((ndir, S5_POWERS, gp), F32)] * 2,
        compiler_params=_cparams(("arbitrary",)),
        name="s5_discretise",
    )(flat(lam_re), flat(lam_im), ldt, tr(b_re), tr(b_im))

    gh = g // S5_HALVES
    out = []
    for di in range(ndir):
        rev = di == 1
        bm, cm, cf = [], [], []
        for hf in range(S5_HALVES):
            gs = slice(hf * gh, (hf + 1) * gh)
            bre = _blockdiag(bb_re[di].reshape(ch, g, p).transpose(1, 0, 2)[gs])
            bim = _blockdiag(bb_im[di].reshape(ch, g, p).transpose(1, 0, 2)[gs])
            bm.append(jnp.concatenate([bre, bim], axis=1))
            cre = _blockdiag(c_re[di].transpose(0, 2, 1)[gs])
            cim = _blockdiag(c_im[di].transpose(0, 2, 1)[gs])
            cm.append(jnp.concatenate([cre, -cim], axis=0))
            cs = slice(hf * gh * p, (hf + 1) * gh * p)
            pwr = jnp.concatenate([pw_re[di][:, cs], pw_im[di][:, cs]], axis=1)
            r = jnp.arange(SUBLANES)[:, None]
            tiles = []
            for kk in range(3):
                sh = 1 << kk
                keep = (r + sh <= SUBLANES - 1) if rev else (r >= sh)
                tiles.append(jnp.where(keep, pwr[sh - 1][None, :], 0.0))
            tiles.append(pwr[::-1] if rev else pwr)
            cf.append(jnp.stack(tiles))
        out.append((jnp.stack(bm).astype(BF16), jnp.stack(cm).astype(BF16), jnp.stack(cf)))
    return out


def _s5_kernel(u_ref, bm_ref, cm_ref, cf_ref, y_ref, xs_ref, carry_ref, *, reverse):
    ns = S5_HALF_STATES
    re = slice(0, ns)
    im = slice(ns, 2 * ns)

    @pl.when(pl.program_id(2) == 0)
    def _():
        carry_ref[...] = jnp.zeros_like(carry_ref)

    xs_ref[...] = _dot(u_ref[0].astype(BF16), bm_ref[0])
    n_tiles = TOKEN_BLOCK // SUBLANES

    def body(i, carry):
        cr, ci = carry
        tile = (n_tiles - 1 - i) if reverse else i
        rows = pl.ds(pl.multiple_of(tile * SUBLANES, SUBLANES), SUBLANES)
        xr = xs_ref[rows, re]
        xi = xs_ref[rows, im]
        for kk in range(3):
            sh = (SUBLANES - (1 << kk)) if reverse else (1 << kk)
            ar = cf_ref[0, kk, :, re]
            ai = cf_ref[0, kk, :, im]
            sr = pltpu.roll(xr, sh, 0)
            si = pltpu.roll(xi, sh, 0)
            xr, xi = xr + ar * sr - ai * si, xi + ar * si + ai * sr
        pr = cf_ref[0, 3, :, re]
        pi = cf_ref[0, 3, :, im]
        xr, xi = xr + pr * cr - pi * ci, xi + pr * ci + pi * cr
        xs_ref[rows, re] = xr
        xs_ref[rows, im] = xi
        last = 0 if reverse else SUBLANES - 1
        return xr[last:last + 1], xi[last:last + 1]

    cr, ci = lax.fori_loop(0, n_tiles, body, (carry_ref[0:1, re], carry_ref[0:1, im]))
    carry_ref[0:1, re] = cr
    carry_ref[0:1, im] = ci
    y_ref[0] = _dot(xs_ref[...].astype(BF16), cm_ref[0])


def _s5_scan(u, bmat, cmat, coef, reverse):
    b, s, _ = u.shape
    tb = TOKEN_BLOCK
    nblk = s // tb

    def blk(c):
        return jnp.where(c == 0, 0, nblk - c) if reverse else c

    return pl.pallas_call(
        functools.partial(_s5_kernel, reverse=reverse),
        grid=(b, S5_HALVES, nblk),
        in_specs=[pl.BlockSpec((1, tb, S5_HALF_IN), lambda bb, hf, c: (bb, blk(c), hf)),
                  pl.BlockSpec((1, S5_HALF_IN, 2 * S5_HALF_STATES), lambda bb, hf, c: (hf, 0, 0)),
                  pl.BlockSpec((1, 2 * S5_HALF_STATES, S5_HALF_IN), lambda bb, hf, c: (hf, 0, 0)),
                  pl.BlockSpec((1, 4, SUBLANES, 2 * S5_HALF_STATES), lambda bb, hf, c: (hf, 0, 0, 0))],
        out_specs=pl.BlockSpec((1, tb, S5_HALF_IN), lambda bb, hf, c: (bb, blk(c), hf)),
        out_shape=jax.ShapeDtypeStruct((b, s, S5_WIDTH), F32),
        scratch_shapes=[pltpu.VMEM((tb, 2 * S5_HALF_STATES), F32),
                        pltpu.VMEM((SUBLANES, 2 * S5_HALF_STATES), F32)],
        compiler_params=_cparams(("parallel", "parallel", "arbitrary")),
        name="s5_scan_rev" if reverse else "s5_scan_fwd",
    )(u, bmat, cmat, coef)


def _even_out_kernel(x_ref, mod_ref, o_ref, gm_ref, u_ref, yf_ref, yb_ref, gs_ref,
                     d_ref, wglu_ref, bglu_ref, wout_ref, out_ref):
    y = u_ref[0] * d_ref[...] + yf_ref[0] + yb_ref[0]
    a = _dot(jax.nn.gelu(y).astype(BF16), wglu_ref[...]) + bglu_ref[...]
    s5o = a[:, :S5_WIDTH] * jax.nn.sigmoid(a[:, S5_WIDTH:])
    b1 = (o_ref[0] * jax.nn.silu(gm_ref[0])).astype(BF16)
    b2 = (s5o * jax.nn.silu(gs_ref[0])).astype(BF16)
    res = _dot(b1, wout_ref[0:MLA_WIDTH, :]) + _dot(b2, wout_ref[MLA_WIDTH:, :])
    out_ref[0] = x_ref[0] + mod_ref[0, 0][2:3] * res


def _even_out(xc, mod, o, gm, u, yf, yb, gs, w):
    b, s, d = xc.shape
    tb = TOKEN_BLOCK
    const = lambda bb, t: (0, 0)
    tok = lambda width: pl.BlockSpec((1, tb, width), lambda bb, t: (bb, t, 0))
    return pl.pallas_call(
        _even_out_kernel,
        grid=(b, s // tb),
        in_specs=[tok(d),
                  pl.BlockSpec((1, 1, 3, d), lambda bb, t: (bb, _seg(t), 0, 0)),
                  tok(MLA_WIDTH), tok(MLA_WIDTH), tok(S5_WIDTH), tok(S5_WIDTH), tok(S5_WIDTH), tok(S5_WIDTH),
                  pl.BlockSpec((1, S5_WIDTH), const),
                  pl.BlockSpec((S5_WIDTH, 2 * S5_WIDTH), const),
                  pl.BlockSpec((1, 2 * S5_WIDTH), const),
                  pl.BlockSpec((MLA_WIDTH + S5_WIDTH, d), const)],
        out_specs=tok(d),
        out_shape=jax.ShapeDtypeStruct((b, s, d), F32),
        compiler_params=_cparams(("parallel", "parallel")),
        name="even_out",
    )(xc, mod, o, gm, u, yf, yb, gs, w["d"], w["wglu"], w["bglu"], w["wout"])


def _odd_in_kernel(x_ref, g_ref, mod_ref, win_ref, q_ref, k_ref, v_ref, gate_ref):
    h = _modnorm(x_ref[0], g_ref[...], mod_ref[0, 0]).astype(BF16)
    z = _dot(h, win_ref[...])
    w = NA_WIDTH
    q_ref[0] = (z[:, :w] * NA_SCALE).astype(BF16)
    k_ref[0] = z[:, w:2 * w].astype(BF16)
    v_ref[0] = z[:, 2 * w:3 * w].astype(BF16)
    gate_ref[0] = z[:, 3 * w:]


def _odd_in(xc, g, mod, win):
    b, s, d = xc.shape
    tb = TOKEN_BLOCK
    const = lambda bb, t: (0, 0)
    tok = lambda width: pl.BlockSpec((1, tb, width), lambda bb, t: (bb, t, 0))
    return pl.pallas_call(
        _odd_in_kernel,
        grid=(b, s // tb),
        in_specs=[tok(d),
                  pl.BlockSpec((1, d), const),
                  pl.BlockSpec((1, 1, 3, d), lambda bb, t: (bb, _seg(t), 0, 0)),
                  pl.BlockSpec((d, 4 * NA_WIDTH), const)],
        out_specs=[tok(NA_WIDTH)] * 4,
        out_shape=[jax.ShapeDtypeStruct((b, s, NA_WIDTH), BF16)] * 3
        + [jax.ShapeDtypeStruct((b, s, NA_WIDTH), F32)],
        compiler_params=_cparams(("parallel", "parallel")),
        name="odd_in",
    )(xc, g, mod, win)


def _na_tables(rpb, n_rows):
    h = rpb.shape[0]
    nblk = n_rows // NA_Q_ROWS
    win_rows = NA_WIN_BLOCKS * NA_Q_ROWS
    col = jnp.arange(GRID_W, dtype=jnp.int32)
    cs = jnp.clip(col - NA_KW // 2, 0, GRID_W - NA_KW)
    colmask = (col[None, :] >= cs[:, None]) & (col[None, :] < cs[:, None] + NA_KW)
    dc = jnp.clip(col[None, :] - col[:, None] + NA_KW - 1, 0, 2 * NA_KW - 2)
    tabs = []
    for i in (0, 1, nblk - 1):
        r = NA_Q_ROWS * i + jnp.arange(NA_Q_ROWS, dtype=jnp.int32)
        ws = NA_Q_ROWS * min(max(i - 1, 0), nblk - NA_WIN_BLOCKS)
        rs = jnp.clip(r - NA_KH // 2, 0, n_rows - NA_KH)
        krow = ws + jnp.arange(win_rows, dtype=jnp.int32)
        rowmask = (krow[None, :] >= rs[:, None]) & (krow[None, :] < rs[:, None] + NA_KH)
        dr = jnp.clip(krow[None, :] - r[:, None] + NA_KH - 1, 0, 2 * NA_KH - 2)
        bias = rpb[:, dr[:, None, :, None], dc[None, :, None, :]]
        mask = rowmask[:, None, :, None] & colmask[None, :, None, :]
        tab = jnp.where(mask[None], bias, NEG_INF)
        tabs.append(tab.reshape(h // 2, 2, NA_Q_ROWS * GRID_W, win_rows * GRID_W))
    return jnp.stack(tabs)


def _na_kernel(q_ref, kc_ref, k0_ref, k1_ref, k2_ref, vc_ref, v0_ref, v1_ref, v2_ref,
               tab_ref, o_ref):
    tb = TOKEN_BLOCK
    lane = lax.broadcasted_iota(jnp.int32, (tb, LANES), 1)

    def attend(with_window):
        krefs = [kc_ref] + ([k0_ref, k1_ref, k2_ref] if with_window else [])
        vrefs = [vc_ref] + ([v0_ref, v1_ref, v2_ref] if with_window else [])
        outs = []
        q32 = q_ref[0].astype(F32)
        for hh in range(2):
            mine = (lane < NA_DH) if hh == 0 else (lane >= NA_DH)
            q = jnp.where(mine, q32, 0.0).astype(BF16)
            ss = []
            for j, kr in enumerate(krefs):
                s = _dot_nt(q, kr[0])
                if j > 0:
                    s = s + tab_ref[0, 0, hh, :, (j - 1) * tb:j * tb]
                ss.append(s)
            m = functools.reduce(jnp.maximum, [jnp.max(s, axis=-1, keepdims=True) for s in ss])
            ps = [jnp.exp(s - m) for s in ss]
            l = functools.reduce(jnp.add, [jnp.sum(p, axis=-1, keepdims=True) for p in ps])
            o = functools.reduce(jnp.add, [_dot(p.astype(BF16), vr[0]) for p, vr in zip(ps, vrefs)])
            outs.append(o / l)
        o_ref[0] = jnp.where(lane < NA_DH, outs[0], outs[1])

    @pl.when(pl.program_id(1) == 0)
    def _():
        attend(False)

    @pl.when(pl.program_id(1) > 0)
    def _():
        attend(True)


def _na(q, k, v, tabs):
    b, s, _ = q.shape
    tb = TOKEN_BLOCK
    nblk = s // tb
    nlat = nblk - 1

    def win(j):
        return lambda hp, t, bb: (bb, 1 + jnp.clip(t - 2, 0, nlat - NA_WIN_BLOCKS) + j, hp)

    def variant(hp, t, bb):
        return (jnp.where(t <= 1, 0, jnp.where(t == nlat, 2, 1)), hp, 0, 0, 0)

    cur = pl.BlockSpec((1, tb, LANES), lambda hp, t, bb: (bb, t, hp))
    ctx = pl.BlockSpec((1, tb, LANES), lambda hp, t, bb: (bb, 0, hp))
    wins = [pl.BlockSpec((1, tb, LANES), win(j)) for j in range(NA_WIN_BLOCKS)]
    return pl.pallas_call(
        _na_kernel,
        grid=(NA_HEADS // 2, nblk, b),
        in_specs=[cur, ctx] + wins + [ctx] + wins
        + [pl.BlockSpec((1, 1, 2, tb, NA_WIN_BLOCKS * tb), variant)],
        out_specs=cur,
        out_shape=jax.ShapeDtypeStruct((b, s, NA_WIDTH), F32),
        compiler_params=_cparams(("parallel", "parallel", "parallel")),
        name="na_attn",
    )(q, k, k, k, k, v, v, v, v, tabs)


def _odd_out_kernel(x_ref, mod_ref, o_ref, gate_ref, wout_ref, out_ref):
    br = (o_ref[0] * jax.nn.silu(gate_ref[0])).astype(BF16)
    out_ref[0] = x_ref[0] + mod_ref[0, 0][2:3] * _dot(br, wout_ref[...])


def _odd_out_final_kernel(x_ref, mod_ref, o_ref, gate_ref, wout_ref, fg_ref, out_ref):
    br = (o_ref[0] * jax.nn.silu(gate_ref[0])).astype(BF16)
    xn = x_ref[0] + mod_ref[0, 0][2:3] * _dot(br, wout_ref[...])
    out_ref[0] = _rms(xn, fg_ref[...])


def _odd_out(xc, mod, o, gate, wout, final_g=None):
    b, s, d = xc.shape
    tb = TOKEN_BLOCK
    const = lambda bb, t: (0, 0)
    final = final_g is not None
    off = 1 if final else 0
    tok = lambda width: pl.BlockSpec((1, tb, width), lambda bb, t: (bb, t + off, 0))
    in_specs = [tok(d),
                pl.BlockSpec((1, 1, 3, d), lambda bb, t: (bb, _seg(t + off), 0, 0)),
                tok(NA_WIDTH), tok(NA_WIDTH),
                pl.BlockSpec((NA_WIDTH, d), const)]
    args = [xc, mod, o, gate, wout]
    if final:
        in_specs.append(pl.BlockSpec((1, d), const))
        args.append(final_g)
    return pl.pallas_call(
        _odd_out_final_kernel if final else _odd_out_kernel,
        grid=(b, s // tb - off),
        in_specs=in_specs,
        out_specs=pl.BlockSpec((1, tb, d), lambda bb, t: (bb, t, 0)),
        out_shape=jax.ShapeDtypeStruct((b, s - off * tb, d), F32),
        compiler_params=_cparams(("parallel", "parallel")),
        name="odd_out_final" if final else "odd_out",
    )(*args)


def _rope_rot(w):
    w4 = w.reshape(w.shape[:-1] + (2, 2, MLA_ROPE // 4))
    return jnp.stack([-w4[..., 1, :], w4[..., 0, :]], axis=-2).reshape(w.shape)


def _rope_tables(n_lat, n_ctx):
    t = jnp.arange(n_lat, dtype=jnp.int32)
    row = (t // GRID_W).astype(F32)
    col = (t % GRID_W).astype(F32)
    nf = MLA_ROPE // 4
    freqs = ROPE_BASE ** (-jnp.arange(nf, dtype=F32) / nf)
    ar = row[:, None] * freqs
    ac = col[:, None] * freqs
    cos = jnp.concatenate([jnp.cos(ar), jnp.cos(ar), jnp.cos(ac), jnp.cos(ac)], axis=-1)
    sin = jnp.concatenate([jnp.sin(ar), jnp.sin(ar), jnp.sin(ac), jnp.sin(ac)], axis=-1)
    pad = LANES - MLA_QK
    cos = jnp.concatenate([jnp.ones((n_lat, MLA_NOPE), F32), cos, jnp.ones((n_lat, pad), F32)], axis=-1)
    sin = jnp.concatenate([jnp.zeros((n_lat, MLA_NOPE), F32), sin, jnp.zeros((n_lat, pad), F32)], axis=-1)
    cos = jnp.concatenate([jnp.ones((n_ctx, LANES), F32), cos], axis=0)
    sin = jnp.concatenate([jnp.zeros((n_ctx, LANES), F32), sin], axis=0)
    return cos, sin


def _even_weights(w_in, q_norm, kv_norm, w_uq, w_ukv, d_skip, w_glu, b_glu, w_out):
    d = w_in.shape[0]
    s1 = MLA_Q_RANK
    s2 = s1 + MLA_KV_RANK
    s3 = s2 + MLA_ROPE
    kr = w_in[:, s2:s3]
    z_lo = jnp.zeros((d, MLA_NOPE), F32)
    z_hi = jnp.zeros((d, LANES - MLA_QK), F32)
    win = jnp.concatenate([w_in[:, :s2], z_lo, kr, z_hi, z_lo, _rope_rot(kr), z_hi, w_in[:, s3:]], axis=1)
    uq = w_uq.reshape(MLA_Q_RANK, MLA_HEADS, MLA_QK)
    zq_lo = jnp.zeros((MLA_Q_RANK, MLA_HEADS, MLA_NOPE), F32)
    zq_hi = jnp.zeros((MLA_Q_RANK, MLA_HEADS, LANES - MLA_QK), F32)
    wuqa = jnp.concatenate([uq, zq_hi], axis=-1).reshape(MLA_Q_RANK, MLA_HEADS * LANES)
    wuqb = jnp.concatenate([zq_lo, _rope_rot(uq[..., MLA_NOPE:]), zq_hi], axis=-1).reshape(MLA_Q_RANK, MLA_HEADS * LANES)
    ukv = w_ukv.reshape(MLA_KV_RANK, MLA_HEADS, MLA_NOPE + MLA_V)
    zk = jnp.zeros((MLA_KV_RANK, MLA_HEADS, LANES - MLA_NOPE), F32)
    wk = jnp.concatenate([ukv[..., :MLA_NOPE], zk], axis=-1).reshape(MLA_KV_RANK, MLA_HEADS * LANES)
    wv = ukv[..., MLA_NOPE:].reshape(MLA_KV_RANK, MLA_WIDTH)
    return dict(win=win.astype(BF16), qn=q_norm[None], kvn=kv_norm[None],
                wuqa=wuqa.astype(BF16), wuqb=wuqb.astype(BF16),
                wukv=jnp.concatenate([wk, wv], axis=1).astype(BF16),
                d=d_skip[None], wglu=w_glu.astype(BF16), bglu=b_glu[None], wout=w_out.astype(BF16))


def kernel(x, c, ctx, c_ctx, ada_w, ada_b, norm_g, final_g, ev_w_in, ev_q_norm, ev_kv_norm, ev_w_uq, ev_w_ukv, s5_lam_re, s5_lam_im, s5_log_dt, s5_b_re, s5_b_im, s5_c_re, s5_c_im, s5_d, s5_w_glu, s5_b_glu, ev_w_out, na_w_in, na_rpb, na_w_out):
    b, n_lat, d = x.shape
    n_ctx = ctx.shape[1]
    depth = ada_w.shape[0]
    assert n_ctx == TOKEN_BLOCK and n_lat % TOKEN_BLOCK == 0
    n_rows = n_lat // GRID_W
    assert n_rows // NA_Q_ROWS >= NA_WIN_BLOCKS + 1
    assert depth % 2 == 0, "the final norm is fused into an odd (last) layer"

    xc = jnp.concatenate([ctx, x], axis=1)
    mods = _adaln(c, c_ctx, ada_w, ada_b)
    cos_t, sin_t = _rope_tables(n_lat, n_ctx)

    for l in range(depth):
        i = l // 2
        g = norm_g[l][None]
        mod = mods[l]
        if l % 2 == 0:
            w = _even_weights(ev_w_in[i], ev_q_norm[i], ev_kv_norm[i], ev_w_uq[i], ev_w_ukv[i],
                              s5_d[i], s5_w_glu[i], s5_b_glu[i], ev_w_out[i])
            s5p = _s5_prepare(s5_lam_re[i], s5_lam_im[i], s5_log_dt[i], s5_b_re[i], s5_b_im[i],
                              s5_c_re[i], s5_c_im[i])
            q, k, v, gm, u, gs = _even_in(xc, g, mod, cos_t, sin_t, w)
            o = _mla(q, k, v)
            yf = _s5_scan(u, *s5p[0], reverse=False)
            yb = _s5_scan(u, *s5p[1], reverse=True)
            xc = _even_out(xc, mod, o, gm, u, yf, yb, gs, w)
        else:
            q, k, v, gate = _odd_in(xc, g, mod, na_w_in[i].astype(BF16))
            o = _na(q, k, v, _na_tables(na_rpb[i], n_rows))
            last = l == depth - 1
            xc = _odd_out(xc, mod, o, gate, na_w_out[i].astype(BF16), final_g[None] if last else None)
    return xc
```

```python
import functools
import math

import jax
import jax.numpy as jnp
from jax import lax
from jax.experimental import pallas as pl
from jax.experimental.pallas import tpu as pltpu

F32 = jnp.float32
BF16 = jnp.bfloat16

SUBLANES = 8
LANES = 128
VMEM_LIMIT_BYTES = 48 * 1024 * 1024

GRID_W = 64
EPS = 1e-6
NEG_INF = -1e30
MLA_HEADS = 8
MLA_NOPE = 64
MLA_ROPE = 32
MLA_V = 64
MLA_QK = MLA_NOPE + MLA_ROPE
MLA_Q_RANK = 256
MLA_KV_RANK = 128
MLA_WIDTH = MLA_HEADS * MLA_V
MLA_SCALE = 1.0 / math.sqrt(MLA_QK)
ROPE_BASE = 10000.0
S5_CH = 16
S5_GROUPS = 32
S5_WIDTH = S5_GROUPS * S5_CH
S5_STATE = 64
S5_HALVES = 2
S5_HALF_IN = S5_WIDTH // S5_HALVES
S5_HALF_STATES = S5_GROUPS * S5_STATE // S5_HALVES
NA_HEADS = 16
NA_DH = 64
NA_WIDTH = NA_HEADS * NA_DH
NA_KH = 8
NA_KW = 16
NA_SCALE = 1.0 / math.sqrt(NA_DH)

TOKEN_BLOCK = 256
S5_TILES = TOKEN_BLOCK // SUBLANES
NA_Q_ROWS = TOKEN_BLOCK // GRID_W
NA_WIN_BLOCKS = 3
MLA_HEADS_PER_STEP = 4
LOG2E = math.log2(math.e)
S5_BATCH_PER_STEP = 2
NA_BATCH_PER_STEP = 2
EVEN_SMALL = MLA_Q_RANK + MLA_KV_RANK + 2 * LANES


def _cparams(sem):
    return pltpu.CompilerParams(dimension_semantics=sem, vmem_limit_bytes=VMEM_LIMIT_BYTES)


def _dot(a, b):
    return jnp.dot(a, b, preferred_element_type=F32)


def _dot_nt(a, b):
    return lax.dot_general(a, b, (((1,), (1,)), ((), ())), preferred_element_type=F32)


def _rms(x, g):
    return x * lax.rsqrt(jnp.mean(x * x, axis=-1, keepdims=True) + EPS) * g


def _modnorm(x, g, mod):
    return _rms(x, g) * (1.0 + mod[1:2]) + mod[0:1]


def _adaln_kernel(c_ref, w_ref, b_ref, o_ref):
    sc = jax.nn.silu(c_ref[...]).astype(BF16)
    o_ref[0] = _dot(sc, w_ref[0].astype(BF16)) + b_ref[0]


def _adaln(c, c_ctx, ada_w, ada_b):
    depth, d, d3 = ada_w.shape
    b = c.shape[0]
    rows = -(-(b + 1) // SUBLANES) * SUBLANES
    call = jnp.concatenate([c, c_ctx[None], jnp.zeros((rows - b - 1, d), F32)], axis=0)
    tn = d
    out = pl.pallas_call(
        _adaln_kernel,
        grid=(depth, d3 // tn),
        in_specs=[pl.BlockSpec((rows, d), lambda l, n: (0, 0)),
                  pl.BlockSpec((1, d, tn), lambda l, n: (l, 0, n)),
                  pl.BlockSpec((1, 1, tn), lambda l, n: (l, 0, n))],
        out_specs=pl.BlockSpec((1, rows, tn), lambda l, n: (l, 0, n)),
        out_shape=jax.ShapeDtypeStruct((depth, rows, d3), F32),
        compiler_params=_cparams(("arbitrary", "arbitrary")),
        name="adaln",
    )(call, ada_w, ada_b[:, None, :])
    lat = out[:, :b].reshape(depth, b, 1, 3, d)
    cx = jnp.broadcast_to(out[:, b].reshape(depth, 1, 1, 3, d), (depth, b, 1, 3, d))
    return jnp.concatenate([cx, lat], axis=2)


def _seg(t):
    return jnp.minimum(t, 1)


def _even_in_kernel(x_ref, g_ref, mod_ref, cos_ref, sin_ref, win_ref, qn_ref, kvn_ref,
                    wuqa_ref, wuqb_ref, wukv_ref, perm_ref,
                    q_ref, k_ref, v_ref, gm_ref, u_ref, gs_ref, up_ref):
    h = _modnorm(x_ref[0], g_ref[...], mod_ref[0, 0]).astype(BF16)
    z = _dot(h, win_ref[...])
    o = EVEN_SMALL
    gm_ref[0] = z[:, o:o + MLA_WIDTH]
    u = z[:, o + MLA_WIDTH:o + MLA_WIDTH + S5_WIDTH]
    u_ref[0] = u
    gs_ref[0] = z[:, o + MLA_WIDTH + S5_WIDTH:]
    up_ref[0] = _dot(perm_ref[...], u.astype(BF16)).astype(BF16)
    cos = cos_ref[...]
    sin = sin_ref[...]
    nq = _rms(z[:, :MLA_Q_RANK], qn_ref[...]).astype(BF16)
    qa = _dot(nq, wuqa_ref[...])
    qb = _dot(nq, wuqb_ref[...])
    nkv = _rms(z[:, MLA_Q_RANK:MLA_Q_RANK + MLA_KV_RANK], kvn_ref[...]).astype(BF16)
    kv = _dot(nkv, wukv_ref[...])
    s0 = MLA_Q_RANK + MLA_KV_RANK
    kr = z[:, s0:s0 + LANES] * cos + z[:, s0 + LANES:s0 + 2 * LANES] * sin
    for hd in range(MLA_HEADS):
        sl = slice(hd * LANES, (hd + 1) * LANES)
        q_ref[0, :, sl] = ((qa[:, sl] * cos + qb[:, sl] * sin) * (MLA_SCALE * LOG2E)).astype(BF16)
        k_ref[0, :, sl] = (kv[:, sl] + kr).astype(BF16)
    v_ref[0] = kv[:, MLA_HEADS * LANES:].astype(BF16)


def _strided_order_matrix():
    row = jnp.arange(TOKEN_BLOCK)
    src = S5_TILES * (row % SUBLANES) + row // SUBLANES
    return (src[:, None] == jnp.arange(TOKEN_BLOCK)[None, :]).astype(BF16)


def _even_in(xc, g, mod, cos_t, sin_t, w):
    b, s, d = xc.shape
    tb = TOKEN_BLOCK
    hs = MLA_HEADS * LANES
    nin = w["win"].shape[1]
    const = lambda bb, t: (0, 0)
    tok = lambda width: pl.BlockSpec((1, tb, width), lambda bb, t: (bb, t, 0))
    return pl.pallas_call(
        _even_in_kernel,
        grid=(b, s // tb),
        in_specs=[tok(d),
                  pl.BlockSpec((1, d), const),
                  pl.BlockSpec((1, 1, 3, d), lambda bb, t: (bb, _seg(t), 0, 0)),
                  pl.BlockSpec((tb, LANES), lambda bb, t: (t, 0)),
                  pl.BlockSpec((tb, LANES), lambda bb, t: (t, 0)),
                  pl.BlockSpec((d, nin), const),
                  pl.BlockSpec((1, MLA_Q_RANK), const),
                  pl.BlockSpec((1, MLA_KV_RANK), const),
                  pl.BlockSpec((MLA_Q_RANK, hs), const),
                  pl.BlockSpec((MLA_Q_RANK, hs), const),
                  pl.BlockSpec((MLA_KV_RANK, hs + MLA_WIDTH), const),
                  pl.BlockSpec((tb, tb), const)],
        out_specs=[tok(hs), tok(hs), tok(MLA_WIDTH), tok(MLA_WIDTH), tok(S5_WIDTH), tok(S5_WIDTH),
                   tok(S5_WIDTH)],
        out_shape=[jax.ShapeDtypeStruct((b, s, hs), BF16),
                   jax.ShapeDtypeStruct((b, s, hs), BF16),
                   jax.ShapeDtypeStruct((b, s, MLA_WIDTH), BF16),
                   jax.ShapeDtypeStruct((b, s, MLA_WIDTH), F32),
                   jax.ShapeDtypeStruct((b, s, S5_WIDTH), F32),
                   jax.ShapeDtypeStruct((b, s, S5_WIDTH), F32),
                   jax.ShapeDtypeStruct((b, s, S5_WIDTH), BF16)],
        compiler_params=_cparams(("parallel", "parallel")),
        name="even_in",
    )(xc, g, mod, cos_t, sin_t, w["win"], w["qn"], w["kvn"], w["wuqa"], w["wuqb"], w["wukv"],
      _strided_order_matrix())


def _mla_kernel(q_ref, k_ref, v_ref, o_ref, *, n_ctx, n_all):
    nh = MLA_HEADS_PER_STEP
    lane = lax.broadcasted_iota(jnp.int32, (TOKEN_BLOCK, nh * MLA_V), 1)

    def attend(nk):
        out = None
        for hh in range(nh):
            sl = slice(hh * LANES, (hh + 1) * LANES)
            s = _dot_nt(q_ref[0, :, sl], k_ref[0, 0:nk, sl])
            p = jnp.exp2(s - jnp.max(s, axis=-1, keepdims=True))
            l = jnp.sum(p, axis=-1, keepdims=True)
            o = _dot(p.astype(BF16), v_ref[0, 0:nk, :]) / l
            out = o if out is None else jnp.where(lane < hh * MLA_V, out, o)
        o_ref[0] = out

    @pl.when(pl.program_id(2) == 0)
    def _():
        attend(n_ctx)

    @pl.when(pl.program_id(2) > 0)
    def _():
        attend(n_all)


def _mla(q, k, v):
    b, s, _ = q.shape
    tb = TOKEN_BLOCK
    nh = MLA_HEADS_PER_STEP
    return pl.pallas_call(
        functools.partial(_mla_kernel, n_ctx=tb, n_all=s),
        grid=(b, MLA_HEADS // nh, s // tb),
        in_specs=[pl.BlockSpec((1, tb, nh * LANES), lambda bb, hg, t: (bb, t, hg)),
                  pl.BlockSpec((1, s, nh * LANES), lambda bb, hg, t: (bb, 0, hg)),
                  pl.BlockSpec((1, s, nh * MLA_V), lambda bb, hg, t: (bb, 0, hg))],
        out_specs=pl.BlockSpec((1, tb, nh * MLA_V), lambda bb, hg, t: (bb, t, hg)),
        out_shape=jax.ShapeDtypeStruct((b, s, MLA_WIDTH), F32),
        compiler_params=_cparams(("parallel", "parallel", "arbitrary")),
        name="mla_attn",
    )(q, k, v)


def _s5_prep_kernel(lre_ref, lim_ref, ldt_ref, bre_ref, bim_ref,
                    bbre_ref, bbim_ref, pwre_ref, pwim_ref, mwre_ref, mwim_ref):
    dt = jnp.exp(ldt_ref[0])
    lr = jnp.minimum(lre_ref[0], -1e-4)
    li = lim_ref[0]
    mag = jnp.exp(lr * dt)
    lbr = mag * jnp.cos(li * dt)
    lbi = mag * jnp.sin(li * dt)
    den = lr * lr + li * li
    nr = lbr - 1.0
    k_re = (nr * lr + lbi * li) / den
    k_im = (lbi * lr - nr * li) / den
    b_re = bre_ref[0]
    b_im = bim_ref[0]
    bbre_ref[0] = k_re * b_re - k_im * b_im
    bbim_ref[0] = k_re * b_im + k_im * b_re
    pr, pi = lbr, lbi
    for j in range(S5_TILES):
        pwre_ref[0, j:j + 1, :] = pr
        pwim_ref[0, j:j + 1, :] = pi
        if j < S5_TILES - 1:
            pr, pi = pr * lbr - pi * lbi, pr * lbi + pi * lbr
    mr, mi = pr, pi
    qr, qi = mr, mi
    for j in range(SUBLANES):
        mwre_ref[0, j:j + 1, :] = qr
        mwim_ref[0, j:j + 1, :] = qi
        qr, qi = qr * mr - qi * mi, qr * mi + qi * mr


def _blockdiag(m):
    n, r, c = m.shape
    eye = jnp.eye(n, dtype=m.dtype)
    return (m[:, :, None, :] * eye[:, None, :, None]).reshape(n * r, n * c)


def _s5_prepare(lam_re, lam_im, log_dt, b_re, b_im, c_re, c_im):
    ndir, g, p = lam_re.shape
    ch = b_re.shape[-1]
    gp = g * p
    flat = lambda a: a.reshape(ndir, 1, gp)
    ldt = flat(jnp.broadcast_to(log_dt[:, :, None], (ndir, g, p)))
    tr = lambda a: a.transpose(0, 3, 1, 2).reshape(ndir, ch, gp)
    row = pl.BlockSpec((1, 1, gp), lambda d: (d, 0, 0))
    mat = pl.BlockSpec((1, ch, gp), lambda d: (d, 0, 0))
    pw = pl.BlockSpec((1, S5_TILES, gp), lambda d: (d, 0, 0))
    mw = pl.BlockSpec((1, SUBLANES, gp), lambda d: (d, 0, 0))
    bb_re, bb_im, pw_re, pw_im, mw_re, mw_im = pl.pallas_call(
        _s5_prep_kernel,
        grid=(ndir,),
        in_specs=[row, row, row, mat, mat],
        out_specs=[mat, mat, pw, pw, mw, mw],
        out_shape=[jax.ShapeDtypeStruct((ndir, ch, gp), F32)] * 2
        + [jax.ShapeDtypeStruct((ndir, S5_TILES, gp), F32)] * 2
        + [jax.ShapeDtypeStruct((ndir, SUBLANES, gp), F32)] * 2,
        compiler_params=_cparams(("arbitrary",)),
        name="s5_discretise",
    )(flat(lam_re), flat(lam_im), ldt, tr(b_re), tr(b_im))

    gh = g // S5_HALVES
    out = []
    for di in range(ndir):
        rev = di == 1
        bm, cm, cf, st = [], [], [], []
        for hf in range(S5_HALVES):
            gs = slice(hf * gh, (hf + 1) * gh)
            bre = _blockdiag(bb_re[di].reshape(ch, g, p).transpose(1, 0, 2)[gs])
            bim = _blockdiag(bb_im[di].reshape(ch, g, p).transpose(1, 0, 2)[gs])
            bm.append(jnp.concatenate([bre, bim], axis=1))
            cre = _blockdiag(c_re[di].transpose(0, 2, 1)[gs])
            cim = _blockdiag(c_im[di].transpose(0, 2, 1)[gs])
            cm.append(jnp.concatenate([cre, -cim], axis=0))
            cs = slice(hf * gh * p, (hf + 1) * gh * p)
            pwr = jnp.concatenate([pw_re[di][:, cs], pw_im[di][:, cs]], axis=1)
            mwr = jnp.concatenate([mw_re[di][:, cs], mw_im[di][:, cs]], axis=1)
            r = jnp.arange(SUBLANES)[:, None]
            tiles = [jnp.broadcast_to(pwr[0][None, :], (SUBLANES, pwr.shape[1]))]
            for kk in range(3):
                sh = 1 << kk
                keep = (r + sh <= SUBLANES - 1) if rev else (r >= sh)
                tiles.append(jnp.where(keep, mwr[sh - 1][None, :], 0.0))
            tiles.append(mwr[::-1] if rev else mwr)
            cf.append(jnp.stack(tiles))
            st.append(pwr[::-1] if rev else pwr)
        out.append((jnp.stack(bm).astype(BF16), jnp.stack(cm).astype(BF16), jnp.stack(cf), jnp.stack(st)))
    return out


def _s5_kernel(up_ref, bm_ref, cm_ref, cf_ref, st_ref, y_ref, xs_ref, xb_ref, ys_ref, carry_ref,
               *, reverse):
    nb = up_ref.shape[0]
    tb = TOKEN_BLOCK

    @pl.when(pl.program_id(2) == 0)
    def _():
        carry_ref[...] = jnp.zeros_like(carry_ref)

    xs_ref[...] = _dot(jnp.concatenate([up_ref[bi] for bi in range(nb)], axis=0), bm_ref[0])
    for bi in range(nb):
        _s5_chunk_scan(bi * tb, bi * SUBLANES, cf_ref, st_ref, xs_ref, xb_ref, carry_ref, reverse)
    y_perm = _dot(xb_ref[...], cm_ref[0])
    nslab = S5_HALF_IN // LANES
    for bi in range(nb):
        for hl in range(nslab):
            ys_ref[bi * nslab + hl] = y_perm[bi * tb:(bi + 1) * tb, hl * LANES:(hl + 1) * LANES]
    for bi in range(nb):
        for hl in range(nslab):
            slab = ys_ref.at[bi * nslab + hl]
            for k in range(S5_TILES):
                start = SUBLANES * SUBLANES * (k % (S5_TILES // SUBLANES)) + k // (S5_TILES // SUBLANES)
                y_ref[bi, k * SUBLANES:(k + 1) * SUBLANES, hl * LANES:(hl + 1) * LANES] = (
                    slab[pl.ds(start, SUBLANES, stride=SUBLANES), :])


def _s5_chunk_scan(row0, carry_row, cf_ref, st_ref, xs_ref, xb_ref, carry_ref, reverse):
    ns = S5_HALF_STATES
    nt = S5_TILES
    re = slice(0, ns)
    im = slice(ns, 2 * ns)
    tile_rows = lambda i: pl.ds(pl.multiple_of(row0 + i * SUBLANES, SUBLANES), SUBLANES)

    lr = cf_ref[0, 0, :, re]
    li = cf_ref[0, 0, :, im]

    def pass1(j, carry):
        pr, pi = carry
        rows = tile_rows((nt - 1 - j) if reverse else j)
        xr = xs_ref[rows, re] + (lr * pr - li * pi)
        xi = xs_ref[rows, im] + (lr * pi + li * pr)
        xs_ref[rows, re] = xr
        xs_ref[rows, im] = xi
        return xr, xi

    zero = jnp.zeros((SUBLANES, ns), F32)
    er, ei = lax.fori_loop(0, nt, pass1, (zero, zero))

    for kk in range(3):
        sh = (SUBLANES - (1 << kk)) if reverse else (1 << kk)
        ar = cf_ref[0, 1 + kk, :, re]
        ai = cf_ref[0, 1 + kk, :, im]
        sr = pltpu.roll(er, sh, 0)
        si = pltpu.roll(ei, sh, 0)
        er, ei = er + ar * sr - ai * si, ei + ar * si + ai * sr
    crow = slice(carry_row, carry_row + 1)
    cin_r = carry_ref[crow, re]
    cin_i = carry_ref[crow, im]
    ar = cf_ref[0, 4, :, re]
    ai = cf_ref[0, 4, :, im]
    er, ei = er + ar * cin_r - ai * cin_i, ei + ar * cin_i + ai * cin_r
    out_row = 0 if reverse else SUBLANES - 1
    carry_ref[crow, re] = er[out_row:out_row + 1]
    carry_ref[crow, im] = ei[out_row:out_row + 1]
    sub = lax.broadcasted_iota(jnp.int32, (SUBLANES, ns), 0)
    first = (sub == SUBLANES - 1) if reverse else (sub == 0)
    sh = (SUBLANES - 1) if reverse else 1
    cr = jnp.where(first, cin_r, pltpu.roll(er, sh, 0))
    ci = jnp.where(first, cin_i, pltpu.roll(ei, sh, 0))

    def pass2(j, _):
        rows16 = pl.ds(pl.multiple_of(row0 + j * 2 * SUBLANES, 2 * SUBLANES), 2 * SUBLANES)
        halves = []
        for t in range(2):
            i = 2 * j + t
            rows = tile_rows(i)
            pr = st_ref[0, pl.ds(i, 1), re]
            pi = st_ref[0, pl.ds(i, 1), im]
            xr = xs_ref[rows, re] + (pr * cr - pi * ci)
            xi = xs_ref[rows, im] + (pr * ci + pi * cr)
            halves.append(jnp.concatenate([xr, xi], axis=1))
        xb_ref[rows16, :] = jnp.concatenate(halves, axis=0).astype(BF16)
        return 0

    lax.fori_loop(0, nt // 2, pass2, 0)


def _s5_scan(up, bmat, cmat, coef, step, reverse):
    b, s, _ = up.shape
    tb = TOKEN_BLOCK
    nblk = s // tb
    nb = S5_BATCH_PER_STEP if b % S5_BATCH_PER_STEP == 0 else 1

    def blk(c):
        return jnp.where(c == 0, 0, nblk - c) if reverse else c

    return pl.pallas_call(
        functools.partial(_s5_kernel, reverse=reverse),
        grid=(b // nb, S5_HALVES, nblk),
        in_specs=[pl.BlockSpec((nb, tb, S5_HALF_IN), lambda bb, hf, c: (bb, blk(c), hf)),
                  pl.BlockSpec((1, S5_HALF_IN, 2 * S5_HALF_STATES), lambda bb, hf, c: (hf, 0, 0)),
                  pl.BlockSpec((1, 2 * S5_HALF_STATES, S5_HALF_IN), lambda bb, hf, c: (hf, 0, 0)),
                  pl.BlockSpec((1, 5, SUBLANES, 2 * S5_HALF_STATES), lambda bb, hf, c: (hf, 0, 0, 0)),
                  pl.BlockSpec((1, S5_TILES, 2 * S5_HALF_STATES), lambda bb, hf, c: (hf, 0, 0))],
        out_specs=pl.BlockSpec((nb, tb, S5_HALF_IN), lambda bb, hf, c: (bb, blk(c), hf)),
        out_shape=jax.ShapeDtypeStruct((b, s, S5_WIDTH), F32),
        scratch_shapes=[pltpu.VMEM((nb * tb, 2 * S5_HALF_STATES), F32),
                        pltpu.VMEM((nb * tb, 2 * S5_HALF_STATES), BF16),
                        pltpu.VMEM((nb * S5_HALF_IN // LANES, tb, LANES), F32),
                        pltpu.VMEM((nb * SUBLANES, 2 * S5_HALF_STATES), F32)],
        compiler_params=_cparams(("parallel", "parallel", "arbitrary")),
        name="s5_scan_rev" if reverse else "s5_scan_fwd",
    )(up, bmat, cmat, coef, step)


def _even_out_kernel(x_ref, mod_ref, o_ref, gm_ref, u_ref, yf_ref, yb_ref, gs_ref,
                     d_ref, wglu_ref, bglu_ref, wout_ref, out_ref):
    y = u_ref[0] * d_ref[...] + yf_ref[0] + yb_ref[0]
    a = _dot(jax.nn.gelu(y).astype(BF16), wglu_ref[...]) + bglu_ref[...]
    s5o = a[:, :S5_WIDTH] * jax.nn.sigmoid(a[:, S5_WIDTH:])
    b1 = (o_ref[0] * jax.nn.silu(gm_ref[0])).astype(BF16)
    b2 = (s5o * jax.nn.silu(gs_ref[0])).astype(BF16)
    res = _dot(b1, wout_ref[0:MLA_WIDTH, :]) + _dot(b2, wout_ref[MLA_WIDTH:, :])
    out_ref[0] = x_ref[0] + mod_ref[0, 0][2:3] * res


def _even_out(xc, mod, o, gm, u, yf, yb, gs, w):
    b, s, d = xc.shape
    tb = TOKEN_BLOCK
    const = lambda bb, t: (0, 0)
    tok = lambda width: pl.BlockSpec((1, tb, width), lambda bb, t: (bb, t, 0))
    return pl.pallas_call(
        _even_out_kernel,
        grid=(b, s // tb),
        in_specs=[tok(d),
                  pl.BlockSpec((1, 1, 3, d), lambda bb, t: (bb, _seg(t), 0, 0)),
                  tok(MLA_WIDTH), tok(MLA_WIDTH), tok(S5_WIDTH), tok(S5_WIDTH), tok(S5_WIDTH), tok(S5_WIDTH),
                  pl.BlockSpec((1, S5_WIDTH), const),
                  pl.BlockSpec((S5_WIDTH, 2 * S5_WIDTH), const),
                  pl.BlockSpec((1, 2 * S5_WIDTH), const),
                  pl.BlockSpec((MLA_WIDTH + S5_WIDTH, d), const)],
        out_specs=tok(d),
        out_shape=jax.ShapeDtypeStruct((b, s, d), F32),
        compiler_params=_cparams(("parallel", "parallel")),
        name="even_out",
    )(xc, mod, o, gm, u, yf, yb, gs, w["d"], w["wglu"], w["bglu"], w["wout"])


def _odd_in_kernel(x_ref, g_ref, mod_ref, win_ref, q_ref, k_ref, v_ref, gate_ref):
    h = _modnorm(x_ref[0], g_ref[...], mod_ref[0, 0]).astype(BF16)
    z = _dot(h, win_ref[...])
    w = NA_WIDTH
    q_ref[0] = (z[:, :w] * NA_SCALE).astype(BF16)
    k_ref[0] = z[:, w:2 * w].astype(BF16)
    v_ref[0] = z[:, 2 * w:3 * w].astype(BF16)
    gate_ref[0] = z[:, 3 * w:]


def _odd_in(xc, g, mod, win):
    b, s, d = xc.shape
    tb = TOKEN_BLOCK
    const = lambda bb, t: (0, 0)
    tok = lambda width: pl.BlockSpec((1, tb, width), lambda bb, t: (bb, t, 0))
    return pl.pallas_call(
        _odd_in_kernel,
        grid=(b, s // tb),
        in_specs=[tok(d),
                  pl.BlockSpec((1, d), const),
                  pl.BlockSpec((1, 1, 3, d), lambda bb, t: (bb, _seg(t), 0, 0)),
                  pl.BlockSpec((d, 4 * NA_WIDTH), const)],
        out_specs=[tok(NA_WIDTH)] * 4,
        out_shape=[jax.ShapeDtypeStruct((b, s, NA_WIDTH), BF16)] * 3
        + [jax.ShapeDtypeStruct((b, s, NA_WIDTH), F32)],
        compiler_params=_cparams(("parallel", "parallel")),
        name="odd_in",
    )(xc, g, mod, win)


def _na_tables(rpb, n_rows):
    h = rpb.shape[0]
    nblk = n_rows // NA_Q_ROWS
    win_rows = NA_WIN_BLOCKS * NA_Q_ROWS
    col = jnp.arange(GRID_W, dtype=jnp.int32)
    cs = jnp.clip(col - NA_KW // 2, 0, GRID_W - NA_KW)
    colmask = (col[None, :] >= cs[:, None]) & (col[None, :] < cs[:, None] + NA_KW)
    padded = jnp.pad(rpb, ((0, 0), (0, 0), (GRID_W, GRID_W)))
    tile = jnp.stack([padded[:, :, GRID_W + NA_KW - 1 - c:2 * GRID_W + NA_KW - 1 - c]
                      for c in range(GRID_W)], axis=2)
    tile = jnp.where(colmask[None, None], tile, NEG_INF)
    neg = jnp.full((h, GRID_W, GRID_W), NEG_INF, F32)
    tabs = []
    for i in (0, 1, nblk - 1):
        ws = NA_Q_ROWS * min(max(i - 1, 0), nblk - NA_WIN_BLOCKS)
        qrows = []
        for a in range(NA_Q_ROWS):
            r = NA_Q_ROWS * i + a
            rs = min(max(r - NA_KH // 2, 0), n_rows - NA_KH)
            pieces = [tile[:, ws + kr - r + NA_KH - 1] if rs <= ws + kr < rs + NA_KH else neg
                      for kr in range(win_rows)]
            qrows.append(jnp.concatenate(pieces, axis=-1))
        tab = jnp.concatenate(qrows, axis=-2)
        tabs.append(tab.reshape(h // 2, 2, NA_Q_ROWS * GRID_W, win_rows * GRID_W))
    return jnp.stack(tabs)


def _na_kernel(q_ref, kc_ref, k0_ref, k1_ref, k2_ref, vc_ref, v0_ref, v1_ref, v2_ref,
               tab_ref, o_ref):
    tb = TOKEN_BLOCK
    lane = lax.broadcasted_iota(jnp.int32, (tb, LANES), 1)

    def attend(with_window):
        krefs = [kc_ref] + ([k0_ref, k1_ref, k2_ref] if with_window else [])
        vrefs = [vc_ref] + ([v0_ref, v1_ref, v2_ref] if with_window else [])
        for bi in range(q_ref.shape[0]):
            outs = []
            q32 = q_ref[bi].astype(F32)
            for hh in range(2):
                mine = (lane < NA_DH) if hh == 0 else (lane >= NA_DH)
                q = jnp.where(mine, q32, 0.0).astype(BF16)
                ss = []
                for j, kr in enumerate(krefs):
                    s = _dot_nt(q, kr[bi])
                    if j > 0:
                        s = s + tab_ref[0, 0, hh, :, (j - 1) * tb:j * tb]
                    ss.append(s)
                m = jnp.max(functools.reduce(jnp.maximum, ss), axis=-1, keepdims=True)
                ps = [jnp.exp(s - m) for s in ss]
                l = jnp.sum(functools.reduce(jnp.add, ps), axis=-1, keepdims=True)
                o = functools.reduce(jnp.add, [_dot(p.astype(BF16), vr[bi]) for p, vr in zip(ps, vrefs)])
                outs.append(o / l)
            o_ref[bi] = jnp.where(lane < NA_DH, outs[0], outs[1])

    @pl.when(pl.program_id(1) == 0)
    def _():
        attend(False)

    @pl.when(pl.program_id(1) > 0)
    def _():
        attend(True)


def _na(q, k, v, tabs):
    b, s, _ = q.shape
    tb = TOKEN_BLOCK
    nblk = s // tb
    nlat = nblk - 1
    nb = NA_BATCH_PER_STEP if b % NA_BATCH_PER_STEP == 0 else 1

    def win(j):
        return lambda hp, t, bb: (bb, 1 + jnp.clip(t - 2, 0, nlat - NA_WIN_BLOCKS) + j, hp)

    def variant(hp, t, bb):
        return (jnp.where(t <= 1, 0, jnp.where(t == nlat, 2, 1)), hp, 0, 0, 0)

    cur = pl.BlockSpec((nb, tb, LANES), lambda hp, t, bb: (bb, t, hp))
    ctx = pl.BlockSpec((nb, tb, LANES), lambda hp, t, bb: (bb, 0, hp))
    wins = [pl.BlockSpec((nb, tb, LANES), win(j)) for j in range(NA_WIN_BLOCKS)]
    return pl.pallas_call(
        _na_kernel,
        grid=(NA_HEADS // 2, nblk, b // nb),
        in_specs=[cur, ctx] + wins + [ctx] + wins
        + [pl.BlockSpec((1, 1, 2, tb, NA_WIN_BLOCKS * tb), variant)],
        out_specs=cur,
        out_shape=jax.ShapeDtypeStruct((b, s, NA_WIDTH), F32),
        compiler_params=_cparams(("parallel", "parallel", "parallel")),
        name="na_attn",
    )(q, k, k, k, k, v, v, v, v, tabs)


def _odd_out_kernel(x_ref, mod_ref, o_ref, gate_ref, wout_ref, out_ref):
    br = (o_ref[0] * jax.nn.silu(gate_ref[0])).astype(BF16)
    out_ref[0] = x_ref[0] + mod_ref[0, 0][2:3] * _dot(br, wout_ref[...])


def _odd_out_final_kernel(x_ref, mod_ref, o_ref, gate_ref, wout_ref, fg_ref, out_ref):
    br = (o_ref[0] * jax.nn.silu(gate_ref[0])).astype(BF16)
    xn = x_ref[0] + mod_ref[0, 0][2:3] * _dot(br, wout_ref[...])
    out_ref[0] = _rms(xn, fg_ref[...])


def _odd_out(xc, mod, o, gate, wout, final_g=None):
    b, s, d = xc.shape
    tb = TOKEN_BLOCK
    const = lambda bb, t: (0, 0)
    final = final_g is not None
    off = 1 if final else 0
    tok = lambda width: pl.BlockSpec((1, tb, width), lambda bb, t: (bb, t + off, 0))
    in_specs = [tok(d),
                pl.BlockSpec((1, 1, 3, d), lambda bb, t: (bb, _seg(t + off), 0, 0)),
                tok(NA_WIDTH), tok(NA_WIDTH),
                pl.BlockSpec((NA_WIDTH, d), const)]
    args = [xc, mod, o, gate, wout]
    if final:
        in_specs.append(pl.BlockSpec((1, d), const))
        args.append(final_g)
    return pl.pallas_call(
        _odd_out_final_kernel if final else _odd_out_kernel,
        grid=(b, s // tb - off),
        in_specs=in_specs,
        out_specs=pl.BlockSpec((1, tb, d), lambda bb, t: (bb, t, 0)),
        out_shape=jax.ShapeDtypeStruct((b, s - off * tb, d), F32),
        compiler_params=_cparams(("parallel", "parallel")),
        name="odd_out_final" if final else "odd_out",
    )(*args)


def _rope_rot(w):
    w4 = w.reshape(w.shape[:-1] + (2, 2, MLA_ROPE // 4))
    return jnp.stack([-w4[..., 1, :], w4[..., 0, :]], axis=-2).reshape(w.shape)


def _rope_tables(n_lat, n_ctx):
    t = jnp.arange(n_lat, dtype=jnp.int32)
    row = (t // GRID_W).astype(F32)
    col = (t % GRID_W).astype(F32)
    nf = MLA_ROPE // 4
    freqs = ROPE_BASE ** (-jnp.arange(nf, dtype=F32) / nf)
    ar = row[:, None] * freqs
    ac = col[:, None] * freqs
    cos = jnp.concatenate([jnp.cos(ar), jnp.cos(ar), jnp.cos(ac), jnp.cos(ac)], axis=-1)
    sin = jnp.concatenate([jnp.sin(ar), jnp.sin(ar), jnp.sin(ac), jnp.sin(ac)], axis=-1)
    pad = LANES - MLA_QK
    cos = jnp.concatenate([jnp.ones((n_lat, MLA_NOPE), F32), cos, jnp.ones((n_lat, pad), F32)], axis=-1)
    sin = jnp.concatenate([jnp.zeros((n_lat, MLA_NOPE), F32), sin, jnp.zeros((n_lat, pad), F32)], axis=-1)
    cos = jnp.concatenate([jnp.ones((n_ctx, LANES), F32), cos], axis=0)
    sin = jnp.concatenate([jnp.zeros((n_ctx, LANES), F32), sin], axis=0)
    return cos, sin


def _even_weights(w_in, q_norm, kv_norm, w_uq, w_ukv, d_skip, w_glu, b_glu, w_out):
    d = w_in.shape[0]
    s1 = MLA_Q_RANK
    s2 = s1 + MLA_KV_RANK
    s3 = s2 + MLA_ROPE
    kr = w_in[:, s2:s3]
    z_lo = jnp.zeros((d, MLA_NOPE), F32)
    z_hi = jnp.zeros((d, LANES - MLA_QK), F32)
    win = jnp.concatenate([w_in[:, :s2], z_lo, kr, z_hi, z_lo, _rope_rot(kr), z_hi, w_in[:, s3:]], axis=1)
    uq = w_uq.reshape(MLA_Q_RANK, MLA_HEADS, MLA_QK)
    zq_lo = jnp.zeros((MLA_Q_RANK, MLA_HEADS, MLA_NOPE), F32)
    zq_hi = jnp.zeros((MLA_Q_RANK, MLA_HEADS, LANES - MLA_QK), F32)
    wuqa = jnp.concatenate([uq, zq_hi], axis=-1).reshape(MLA_Q_RANK, MLA_HEADS * LANES)
    wuqb = jnp.concatenate([zq_lo, _rope_rot(uq[..., MLA_NOPE:]), zq_hi], axis=-1).reshape(MLA_Q_RANK, MLA_HEADS * LANES)
    ukv = w_ukv.reshape(MLA_KV_RANK, MLA_HEADS, MLA_NOPE + MLA_V)
    zk = jnp.zeros((MLA_KV_RANK, MLA_HEADS, LANES - MLA_NOPE), F32)
    wk = jnp.concatenate([ukv[..., :MLA_NOPE], zk], axis=-1).reshape(MLA_KV_RANK, MLA_HEADS * LANES)
    wv = ukv[..., MLA_NOPE:].reshape(MLA_KV_RANK, MLA_WIDTH)
    return dict(win=win.astype(BF16), qn=q_norm[None], kvn=kv_norm[None],
                wuqa=wuqa.astype(BF16), wuqb=wuqb.astype(BF16),
                wukv=jnp.concatenate([wk, wv], axis=1).astype(BF16),
                d=d_skip[None], wglu=w_glu.astype(BF16), bglu=b_glu[None], wout=w_out.astype(BF16))


def kernel(x, c, ctx, c_ctx, ada_w, ada_b, norm_g, final_g, ev_w_in, ev_q_norm, ev_kv_norm, ev_w_uq, ev_w_ukv, s5_lam_re, s5_lam_im, s5_log_dt, s5_b_re, s5_b_im, s5_c_re, s5_c_im, s5_d, s5_w_glu, s5_b_glu, ev_w_out, na_w_in, na_rpb, na_w_out):
    b, n_lat, d = x.shape
    n_ctx = ctx.shape[1]
    depth = ada_w.shape[0]
    assert n_ctx == TOKEN_BLOCK and n_lat % TOKEN_BLOCK == 0
    n_rows = n_lat // GRID_W
    assert n_rows // NA_Q_ROWS >= NA_WIN_BLOCKS + 1
    assert depth % 2 == 0, "the final norm is fused into an odd (last) layer"

    xc = jnp.concatenate([ctx, x], axis=1)
    mods = _adaln(c, c_ctx, ada_w, ada_b)
    cos_t, sin_t = _rope_tables(n_lat, n_ctx)

    for l in range(depth):
        i = l // 2
        g = norm_g[l][None]
        mod = mods[l]
        if l % 2 == 0:
            w = _even_weights(ev_w_in[i], ev_q_norm[i], ev_kv_norm[i], ev_w_uq[i], ev_w_ukv[i],
                              s5_d[i], s5_w_glu[i], s5_b_glu[i], ev_w_out[i])
            s5p = _s5_prepare(s5_lam_re[i], s5_lam_im[i], s5_log_dt[i], s5_b_re[i], s5_b_im[i],
                              s5_c_re[i], s5_c_im[i])
            q, k, v, gm, u, gs, up = _even_in(xc, g, mod, cos_t, sin_t, w)
            o = _mla(q, k, v)
            yf = _s5_scan(up, *s5p[0], reverse=False)
            yb = _s5_scan(up, *s5p[1], reverse=True)
            xc = _even_out(xc, mod, o, gm, u, yf, yb, gs, w)
        else:
            q, k, v, gate = _odd_in(xc, g, mod, na_w_in[i].astype(BF16))
            o = _na(q, k, v, _na_tables(na_rpb[i], n_rows))
            last = l == depth - 1
            xc = _odd_out(xc, mod, o, gate, na_w_out[i].astype(BF16), final_g[None] if last else None)
    return xc
```

```python
import functools
import math

import jax
import jax.numpy as jnp
from jax import lax
from jax.experimental import pallas as pl
from jax.experimental.pallas import tpu as pltpu

F32 = jnp.float32
BF16 = jnp.bfloat16

SUBLANES = 8
LANES = 128
VMEM_LIMIT_BYTES = 48 * 1024 * 1024

GRID_W = 64
EPS = 1e-6
NEG_INF = -1e30
MLA_HEADS = 8
MLA_NOPE = 64
MLA_ROPE = 32
MLA_V = 64
MLA_QK = MLA_NOPE + MLA_ROPE
MLA_Q_RANK = 256
MLA_KV_RANK = 128
MLA_WIDTH = MLA_HEADS * MLA_V
MLA_SCALE = 1.0 / math.sqrt(MLA_QK)
ROPE_BASE = 10000.0
S5_CH = 16
S5_GROUPS = 32
S5_WIDTH = S5_GROUPS * S5_CH
S5_STATE = 64
S5_HALVES = 2
S5_HALF_IN = S5_WIDTH // S5_HALVES
S5_HALF_STATES = S5_GROUPS * S5_STATE // S5_HALVES
NA_HEADS = 16
NA_DH = 64
NA_WIDTH = NA_HEADS * NA_DH
NA_KH = 8
NA_KW = 16
NA_SCALE = 1.0 / math.sqrt(NA_DH)

TOKEN_BLOCK = 256
S5_TILES = TOKEN_BLOCK // SUBLANES
NA_Q_ROWS = TOKEN_BLOCK // GRID_W
NA_WIN_BLOCKS = 3
MLA_HEADS_PER_STEP = 2
MLA_SLAB_HEADS = 4
LOG2E = math.log2(math.e)
S5_BATCH_PER_STEP = 2
NA_BATCH_PER_STEP = 8
EVEN_SMALL = MLA_Q_RANK + MLA_KV_RANK + 2 * LANES


def _cparams(sem):
    return pltpu.CompilerParams(dimension_semantics=sem, vmem_limit_bytes=VMEM_LIMIT_BYTES)


def _dot(a, b):
    return jnp.dot(a, b, preferred_element_type=F32)


def _dot_nt(a, b):
    return lax.dot_general(a, b, (((1,), (1,)), ((), ())), preferred_element_type=F32)


def _rms(x, g):
    return x * lax.rsqrt(jnp.mean(x * x, axis=-1, keepdims=True) + EPS) * g


def _modnorm(x, g, mod):
    return _rms(x, g) * (1.0 + mod[1:2]) + mod[0:1]


def _adaln_kernel(c_ref, w_ref, b_ref, o_ref):
    sc = jax.nn.silu(c_ref[...]).astype(BF16)
    o_ref[0] = _dot(sc, w_ref[0].astype(BF16)) + b_ref[0]


def _adaln(c, c_ctx, ada_w, ada_b):
    depth, d, d3 = ada_w.shape
    b = c.shape[0]
    rows = -(-(b + 1) // SUBLANES) * SUBLANES
    call = jnp.concatenate([c, c_ctx[None], jnp.zeros((rows - b - 1, d), F32)], axis=0)
    tn = d
    out = pl.pallas_call(
        _adaln_kernel,
        grid=(depth, d3 // tn),
        in_specs=[pl.BlockSpec((rows, d), lambda l, n: (0, 0)),
                  pl.BlockSpec((1, d, tn), lambda l, n: (l, 0, n)),
                  pl.BlockSpec((1, 1, tn), lambda l, n: (l, 0, n))],
        out_specs=pl.BlockSpec((1, rows, tn), lambda l, n: (l, 0, n)),
        out_shape=jax.ShapeDtypeStruct((depth, rows, d3), F32),
        compiler_params=_cparams(("arbitrary", "arbitrary")),
        name="adaln",
    )(call, ada_w, ada_b[:, None, :])
    lat = out[:, :b].reshape(depth, b, 1, 3, d)
    cx = jnp.broadcast_to(out[:, b].reshape(depth, 1, 1, 3, d), (depth, b, 1, 3, d))
    return jnp.concatenate([cx, lat], axis=2)


def _seg(t):
    return jnp.minimum(t, 1)


def _even_in_kernel(x_ref, g_ref, mod_ref, cos_ref, sin_ref, win_ref, qn_ref, kvn_ref,
                    wuqa_ref, wuqb_ref, wukv_ref, perm_ref,
                    q_ref, k_ref, v_ref, gm_ref, u_ref, gs_ref, up_ref):
    h = _modnorm(x_ref[0], g_ref[...], mod_ref[0, 0]).astype(BF16)
    z = _dot(h, win_ref[...])
    o = EVEN_SMALL
    gm_ref[0] = z[:, o:o + MLA_WIDTH]
    u = z[:, o + MLA_WIDTH:o + MLA_WIDTH + S5_WIDTH]
    u_ref[0] = u
    gs_ref[0] = z[:, o + MLA_WIDTH + S5_WIDTH:]
    up_ref[0] = _dot(perm_ref[...], u.astype(BF16)).astype(BF16)
    cos = cos_ref[...]
    sin = sin_ref[...]
    nq = _rms(z[:, :MLA_Q_RANK], qn_ref[...]).astype(BF16)
    qa = _dot(nq, wuqa_ref[...])
    qb = _dot(nq, wuqb_ref[...])
    nkv = _rms(z[:, MLA_Q_RANK:MLA_Q_RANK + MLA_KV_RANK], kvn_ref[...]).astype(BF16)
    kv = _dot(nkv, wukv_ref[...])
    s0 = MLA_Q_RANK + MLA_KV_RANK
    kr = z[:, s0:s0 + LANES] * cos + z[:, s0 + LANES:s0 + 2 * LANES] * sin
    for hd in range(MLA_HEADS):
        sl = slice(hd * LANES, (hd + 1) * LANES)
        q_ref[0, :, sl] = ((qa[:, sl] * cos + qb[:, sl] * sin) * (MLA_SCALE * LOG2E)).astype(BF16)
        k_ref[0, :, sl] = (kv[:, sl] + kr).astype(BF16)
    v_ref[0] = kv[:, MLA_HEADS * LANES:].astype(BF16)


def _strided_order_matrix():
    row = jnp.arange(TOKEN_BLOCK)
    src = S5_TILES * (row % SUBLANES) + row // SUBLANES
    return (src[:, None] == jnp.arange(TOKEN_BLOCK)[None, :]).astype(BF16)


def _even_in(xc, g, mod, cos_t, sin_t, w):
    b, s, d = xc.shape
    tb = TOKEN_BLOCK
    hs = MLA_HEADS * LANES
    nin = w["win"].shape[1]
    const = lambda bb, t: (0, 0)
    tok = lambda width: pl.BlockSpec((1, tb, width), lambda bb, t: (bb, t, 0))
    return pl.pallas_call(
        _even_in_kernel,
        grid=(b, s // tb),
        in_specs=[tok(d),
                  pl.BlockSpec((1, d), const),
                  pl.BlockSpec((1, 1, 3, d), lambda bb, t: (bb, _seg(t), 0, 0)),
                  pl.BlockSpec((tb, LANES), lambda bb, t: (t, 0)),
                  pl.BlockSpec((tb, LANES), lambda bb, t: (t, 0)),
                  pl.BlockSpec((d, nin), const),
                  pl.BlockSpec((1, MLA_Q_RANK), const),
                  pl.BlockSpec((1, MLA_KV_RANK), const),
                  pl.BlockSpec((MLA_Q_RANK, hs), const),
                  pl.BlockSpec((MLA_Q_RANK, hs), const),
                  pl.BlockSpec((MLA_KV_RANK, hs + MLA_WIDTH), const),
                  pl.BlockSpec((tb, tb), const)],
        out_specs=[tok(hs), tok(hs), tok(MLA_WIDTH), tok(MLA_WIDTH), tok(S5_WIDTH), tok(S5_WIDTH),
                   tok(S5_WIDTH)],
        out_shape=[jax.ShapeDtypeStruct((b, s, hs), BF16),
                   jax.ShapeDtypeStruct((b, s, hs), BF16),
                   jax.ShapeDtypeStruct((b, s, MLA_WIDTH), BF16),
                   jax.ShapeDtypeStruct((b, s, MLA_WIDTH), F32),
                   jax.ShapeDtypeStruct((b, s, S5_WIDTH), F32),
                   jax.ShapeDtypeStruct((b, s, S5_WIDTH), F32),
                   jax.ShapeDtypeStruct((b, s, S5_WIDTH), BF16)],
        compiler_params=_cparams(("parallel", "parallel")),
        name="even_in",
    )(xc, g, mod, cos_t, sin_t, w["win"], w["qn"], w["kvn"], w["wuqa"], w["wuqb"], w["wukv"],
      _strided_order_matrix())


def _mla_kernel(qc_ref, qa_ref, qb_ref, k_ref, v_ref, oc_ref, ol_ref, p_ref, *, n_ctx, n_all):
    nh = MLA_HEADS_PER_STEP
    upper = pl.program_id(1) % (MLA_SLAB_HEADS // nh) == 1

    def attend(q_of_head, nq, nk, out_ref):
        lane = lax.broadcasted_iota(jnp.int32, (nq, nh * MLA_V), 1)
        sums = []
        for hh in range(nh):
            sl = slice(hh * LANES, (hh + 1) * LANES)
            s = _dot_nt(q_of_head(sl), k_ref[0, 0:nk, sl])
            p = jnp.exp2(s - jnp.max(s, axis=-1, keepdims=True))
            sums.append(jnp.sum(p, axis=-1, keepdims=True))
            p_ref[hh * nq:(hh + 1) * nq, 0:nk] = p.astype(BF16)
        o_all = _dot(p_ref[0:nh * nq, 0:nk], v_ref[0, 0:nk, :])
        o_all = jnp.where(upper, o_all[:, nh * MLA_V:], o_all[:, :nh * MLA_V])
        out = None
        for hh in range(nh):
            o = o_all[hh * nq:(hh + 1) * nq] / sums[hh]
            out = o if out is None else jnp.where(lane < hh * MLA_V, out, o)
        out_ref[0] = out

    @pl.when(pl.program_id(2) == 0)
    def _():
        attend(lambda sl: qc_ref[0, :, sl], TOKEN_BLOCK, n_ctx, oc_ref)

    attend(lambda sl: jnp.concatenate([qa_ref[0, :, sl], qb_ref[0, :, sl]], axis=0),
           2 * TOKEN_BLOCK, n_all, ol_ref)


def _mla(q, k, v):
    b, s, _ = q.shape
    tb = TOKEN_BLOCK
    nh = MLA_HEADS_PER_STEP
    n_lat = s - tb
    assert n_lat % (2 * tb) == 0
    qspec = lambda f: pl.BlockSpec((1, tb, nh * LANES), lambda bb, hg, t: (bb, f(t), hg))
    return pl.pallas_call(
        functools.partial(_mla_kernel, n_ctx=tb, n_all=s),
        grid=(b, MLA_HEADS // nh, n_lat // (2 * tb)),
        in_specs=[qspec(lambda t: 0), qspec(lambda t: 1 + 2 * t), qspec(lambda t: 2 + 2 * t),
                  pl.BlockSpec((1, s, nh * LANES), lambda bb, hg, t: (bb, 0, hg)),
                  pl.BlockSpec((1, s, MLA_SLAB_HEADS * MLA_V),
                               lambda bb, hg, t: (bb, 0, hg // (MLA_SLAB_HEADS // nh)))],
        out_specs=[pl.BlockSpec((1, tb, nh * MLA_V), lambda bb, hg, t: (bb, 0, hg)),
                   pl.BlockSpec((1, 2 * tb, nh * MLA_V), lambda bb, hg, t: (bb, t, hg))],
        out_shape=[jax.ShapeDtypeStruct((b, tb, MLA_WIDTH), F32),
                   jax.ShapeDtypeStruct((b, n_lat, MLA_WIDTH), F32)],
        scratch_shapes=[pltpu.VMEM((nh * 2 * tb, s), BF16)],
        compiler_params=_cparams(("parallel", "parallel", "arbitrary")),
        name="mla_attn",
    )(q, q, q, k, v)


def _s5_prep_kernel(lre_ref, lim_ref, ldt_ref, bre_ref, bim_ref,
                    bbre_ref, bbim_ref, pwre_ref, pwim_ref, mwre_ref, mwim_ref):
    dt = jnp.exp(ldt_ref[0])
    lr = jnp.minimum(lre_ref[0], -1e-4)
    li = lim_ref[0]
    mag = jnp.exp(lr * dt)
    lbr = mag * jnp.cos(li * dt)
    lbi = mag * jnp.sin(li * dt)
    den = lr * lr + li * li
    nr = lbr - 1.0
    k_re = (nr * lr + lbi * li) / den
    k_im = (lbi * lr - nr * li) / den
    b_re = bre_ref[0]
    b_im = bim_ref[0]
    bbre_ref[0] = k_re * b_re - k_im * b_im
    bbim_ref[0] = k_re * b_im + k_im * b_re
    pr, pi = lbr, lbi
    for j in range(S5_TILES):
        pwre_ref[0, j:j + 1, :] = pr
        pwim_ref[0, j:j + 1, :] = pi
        if j < S5_TILES - 1:
            pr, pi = pr * lbr - pi * lbi, pr * lbi + pi * lbr
    mr, mi = pr, pi
    qr, qi = mr, mi
    for j in range(SUBLANES):
        mwre_ref[0, j:j + 1, :] = qr
        mwim_ref[0, j:j + 1, :] = qi
        qr, qi = qr * mr - qi * mi, qr * mi + qi * mr


def _blockdiag(m):
    n, r, c = m.shape
    eye = jnp.eye(n, dtype=m.dtype)
    return (m[:, :, None, :] * eye[:, None, :, None]).reshape(n * r, n * c)


def _s5_prepare(lam_re, lam_im, log_dt, b_re, b_im, c_re, c_im):
    ndir, g, p = lam_re.shape
    ch = b_re.shape[-1]
    gp = g * p
    flat = lambda a: a.reshape(ndir, 1, gp)
    ldt = flat(jnp.broadcast_to(log_dt[:, :, None], (ndir, g, p)))
    tr = lambda a: a.transpose(0, 3, 1, 2).reshape(ndir, ch, gp)
    row = pl.BlockSpec((1, 1, gp), lambda d: (d, 0, 0))
    mat = pl.BlockSpec((1, ch, gp), lambda d: (d, 0, 0))
    pw = pl.BlockSpec((1, S5_TILES, gp), lambda d: (d, 0, 0))
    mw = pl.BlockSpec((1, SUBLANES, gp), lambda d: (d, 0, 0))
    bb_re, bb_im, pw_re, pw_im, mw_re, mw_im = pl.pallas_call(
        _s5_prep_kernel,
        grid=(ndir,),
        in_specs=[row, row, row, mat, mat],
        out_specs=[mat, mat, pw, pw, mw, mw],
        out_shape=[jax.ShapeDtypeStruct((ndir, ch, gp), F32)] * 2
        + [jax.ShapeDtypeStruct((ndir, S5_TILES, gp), F32)] * 2
        + [jax.ShapeDtypeStruct((ndir, SUBLANES, gp), F32)] * 2,
        compiler_params=_cparams(("arbitrary",)),
        name="s5_discretise",
    )(flat(lam_re), flat(lam_im), ldt, tr(b_re), tr(b_im))

    gh = g // S5_HALVES
    out = []
    for di in range(ndir):
        rev = di == 1
        bm, cm, cf, st = [], [], [], []
        for hf in range(S5_HALVES):
            gs = slice(hf * gh, (hf + 1) * gh)
            bre = _blockdiag(bb_re[di].reshape(ch, g, p).transpose(1, 0, 2)[gs])
            bim = _blockdiag(bb_im[di].reshape(ch, g, p).transpose(1, 0, 2)[gs])
            bm.append(jnp.concatenate([bre, bim], axis=1))
            cre = _blockdiag(c_re[di].transpose(0, 2, 1)[gs])
            cim = _blockdiag(c_im[di].transpose(0, 2, 1)[gs])
            cm.append(jnp.concatenate([cre, -cim], axis=0))
            cs = slice(hf * gh * p, (hf + 1) * gh * p)
            pwr = jnp.concatenate([pw_re[di][:, cs], pw_im[di][:, cs]], axis=1)
            mwr = jnp.concatenate([mw_re[di][:, cs], mw_im[di][:, cs]], axis=1)
            r = jnp.arange(SUBLANES)[:, None]
            tiles = [jnp.broadcast_to(pwr[0][None, :], (SUBLANES, pwr.shape[1]))]
            for kk in range(3):
                sh = 1 << kk
                keep = (r + sh <= SUBLANES - 1) if rev else (r >= sh)
                tiles.append(jnp.where(keep, mwr[sh - 1][None, :], 0.0))
            tiles.append(mwr[::-1] if rev else mwr)
            cf.append(jnp.stack(tiles))
            st.append(jnp.broadcast_to((pwr[::-1] if rev else pwr)[:, None, :],
                                       (S5_TILES, SUBLANES, pwr.shape[1])))
        out.append((jnp.stack(bm).astype(BF16), jnp.stack(cm).astype(BF16), jnp.stack(cf), jnp.stack(st)))
    return out


def _s5_kernel(up_ref, bm_ref, cm_ref, cf_ref, st_ref, y_ref, xs_ref, xb_ref, ys_ref, carry_ref,
               *, reverse):
    nb = up_ref.shape[0]
    tb = TOKEN_BLOCK

    @pl.when(pl.program_id(2) == 0)
    def _():
        carry_ref[...] = jnp.zeros_like(carry_ref)

    for bi in range(nb):
        xs_ref[bi * tb:(bi + 1) * tb, :] = _dot(up_ref[bi], bm_ref[0])
    for bi in range(nb):
        _s5_chunk_scan(bi * tb, bi * SUBLANES, cf_ref, st_ref, xs_ref, xb_ref, carry_ref, reverse)
    y_perm = jnp.concatenate([_dot(xb_ref[bi * tb:(bi + 1) * tb, :], cm_ref[0]) for bi in range(nb)], axis=0)
    nslab = S5_HALF_IN // LANES
    for bi in range(nb):
        for hl in range(nslab):
            ys_ref[bi * nslab + hl] = y_perm[bi * tb:(bi + 1) * tb, hl * LANES:(hl + 1) * LANES]
    for bi in range(nb):
        for hl in range(nslab):
            slab = ys_ref.at[bi * nslab + hl]
            for k in range(S5_TILES):
                start = SUBLANES * SUBLANES * (k % (S5_TILES // SUBLANES)) + k // (S5_TILES // SUBLANES)
                y_ref[bi, k * SUBLANES:(k + 1) * SUBLANES, hl * LANES:(hl + 1) * LANES] = (
                    slab[pl.ds(start, SUBLANES, stride=SUBLANES), :])


def _s5_chunk_scan(row0, carry_row, cf_ref, st_ref, xs_ref, xb_ref, carry_ref, reverse):
    ns = S5_HALF_STATES
    nt = S5_TILES
    re = slice(0, ns)
    im = slice(ns, 2 * ns)
    tile_rows = lambda i: pl.ds(pl.multiple_of(row0 + i * SUBLANES, SUBLANES), SUBLANES)

    lr = cf_ref[0, 0, :, re]
    li = cf_ref[0, 0, :, im]

    def pass1(j, carry):
        pr, pi = carry
        rows = tile_rows((nt - 1 - j) if reverse else j)
        xr = xs_ref[rows, re] + (lr * pr - li * pi)
        xi = xs_ref[rows, im] + (lr * pi + li * pr)
        xs_ref[rows, re] = xr
        xs_ref[rows, im] = xi
        return xr, xi

    zero = jnp.zeros((SUBLANES, ns), F32)
    er, ei = lax.fori_loop(0, nt, pass1, (zero, zero), unroll=2)

    for kk in range(3):
        sh = (SUBLANES - (1 << kk)) if reverse else (1 << kk)
        ar = cf_ref[0, 1 + kk, :, re]
        ai = cf_ref[0, 1 + kk, :, im]
        sr = pltpu.roll(er, sh, 0)
        si = pltpu.roll(ei, sh, 0)
        er, ei = er + ar * sr - ai * si, ei + ar * si + ai * sr
    crow = slice(carry_row, carry_row + 1)
    cin_r = carry_ref[crow, re]
    cin_i = carry_ref[crow, im]
    ar = cf_ref[0, 4, :, re]
    ai = cf_ref[0, 4, :, im]
    er, ei = er + ar * cin_r - ai * cin_i, ei + ar * cin_i + ai * cin_r
    out_row = 0 if reverse else SUBLANES - 1
    carry_ref[crow, re] = er[out_row:out_row + 1]
    carry_ref[crow, im] = ei[out_row:out_row + 1]
    sub = lax.broadcasted_iota(jnp.int32, (SUBLANES, ns), 0)
    first = (sub == SUBLANES - 1) if reverse else (sub == 0)
    sh = (SUBLANES - 1) if reverse else 1
    cr = jnp.where(first, cin_r, pltpu.roll(er, sh, 0))
    ci = jnp.where(first, cin_i, pltpu.roll(ei, sh, 0))

    def pass2(j, _):
        rows16 = pl.ds(pl.multiple_of(row0 + j * 2 * SUBLANES, 2 * SUBLANES), 2 * SUBLANES)
        halves = []
        for t in range(2):
            i = 2 * j + t
            rows = tile_rows(i)
            pr = st_ref[0, i, :, re]
            pi = st_ref[0, i, :, im]
            xr = xs_ref[rows, re] + (pr * cr - pi * ci)
            xi = xs_ref[rows, im] + (pr * ci + pi * cr)
            halves.append(jnp.concatenate([xr, xi], axis=1))
        xb_ref[rows16, :] = jnp.concatenate(halves, axis=0).astype(BF16)
        return 0

    lax.fori_loop(0, nt // 2, pass2, 0)


def _s5_scan(up, bmat, cmat, coef, step, reverse):
    b, s, _ = up.shape
    tb = TOKEN_BLOCK
    nblk = s // tb
    nb = S5_BATCH_PER_STEP if b % S5_BATCH_PER_STEP == 0 else 1

    def blk(c):
        return jnp.where(c == 0, 0, nblk - c) if reverse else c

    return pl.pallas_call(
        functools.partial(_s5_kernel, reverse=reverse),
        grid=(b // nb, S5_HALVES, nblk),
        in_specs=[pl.BlockSpec((nb, tb, S5_HALF_IN), lambda bb, hf, c: (bb, blk(c), hf)),
                  pl.BlockSpec((1, S5_HALF_IN, 2 * S5_HALF_STATES), lambda bb, hf, c: (hf, 0, 0)),
                  pl.BlockSpec((1, 2 * S5_HALF_STATES, S5_HALF_IN), lambda bb, hf, c: (hf, 0, 0)),
                  pl.BlockSpec((1, 5, SUBLANES, 2 * S5_HALF_STATES), lambda bb, hf, c: (hf, 0, 0, 0)),
                  pl.BlockSpec((1, S5_TILES, SUBLANES, 2 * S5_HALF_STATES), lambda bb, hf, c: (hf, 0, 0, 0))],
        out_specs=pl.BlockSpec((nb, tb, S5_HALF_IN), lambda bb, hf, c: (bb, blk(c), hf)),
        out_shape=jax.ShapeDtypeStruct((b, s, S5_WIDTH), F32),
        scratch_shapes=[pltpu.VMEM((nb * tb, 2 * S5_HALF_STATES), F32),
                        pltpu.VMEM((nb * tb, 2 * S5_HALF_STATES), BF16),
                        pltpu.VMEM((nb * S5_HALF_IN // LANES, tb, LANES), F32),
                        pltpu.VMEM((nb * SUBLANES, 2 * S5_HALF_STATES), F32)],
        compiler_params=_cparams(("parallel", "parallel", "arbitrary")),
        name="s5_scan_rev" if reverse else "s5_scan_fwd",
    )(up, bmat, cmat, coef, step)


def _even_out_kernel(x_ref, mod_ref, oc_ref, ol_ref, gm_ref, u_ref, yf_ref, yb_ref, gs_ref,
                     d_ref, wglu_ref, bglu_ref, wout_ref, out_ref):
    y = u_ref[0] * d_ref[...] + yf_ref[0] + yb_ref[0]
    a = _dot(jax.nn.gelu(y).astype(BF16), wglu_ref[...]) + bglu_ref[...]
    s5o = a[:, :S5_WIDTH] * jax.nn.sigmoid(a[:, S5_WIDTH:])
    o = jnp.where(pl.program_id(1) == 0, oc_ref[0], ol_ref[0])
    b1 = (o * jax.nn.silu(gm_ref[0])).astype(BF16)
    b2 = (s5o * jax.nn.silu(gs_ref[0])).astype(BF16)
    res = _dot(b1, wout_ref[0:MLA_WIDTH, :]) + _dot(b2, wout_ref[MLA_WIDTH:, :])
    out_ref[0] = x_ref[0] + mod_ref[0, 0][2:3] * res


def _even_out(xc, mod, o_ctx, o_lat, gm, u, yf, yb, gs, w):
    b, s, d = xc.shape
    tb = TOKEN_BLOCK
    const = lambda bb, t: (0, 0)
    tok = lambda width: pl.BlockSpec((1, tb, width), lambda bb, t: (bb, t, 0))
    return pl.pallas_call(
        _even_out_kernel,
        grid=(b, s // tb),
        in_specs=[tok(d),
                  pl.BlockSpec((1, 1, 3, d), lambda bb, t: (bb, _seg(t), 0, 0)),
                  pl.BlockSpec((1, tb, MLA_WIDTH), lambda bb, t: (bb, 0, 0)),
                  pl.BlockSpec((1, tb, MLA_WIDTH), lambda bb, t: (bb, jnp.maximum(t - 1, 0), 0)),
                  tok(MLA_WIDTH), tok(S5_WIDTH), tok(S5_WIDTH), tok(S5_WIDTH), tok(S5_WIDTH),
                  pl.BlockSpec((1, S5_WIDTH), const),
                  pl.BlockSpec((S5_WIDTH, 2 * S5_WIDTH), const),
                  pl.BlockSpec((1, 2 * S5_WIDTH), const),
                  pl.BlockSpec((MLA_WIDTH + S5_WIDTH, d), const)],
        out_specs=tok(d),
        out_shape=jax.ShapeDtypeStruct((b, s, d), F32),
        compiler_params=_cparams(("parallel", "parallel")),
        name="even_out",
    )(xc, mod, o_ctx, o_lat, gm, u, yf, yb, gs, w["d"], w["wglu"], w["bglu"], w["wout"])


def _odd_in_kernel(x_ref, g_ref, mod_ref, win_ref, q_ref, k_ref, v_ref, gate_ref):
    h = _modnorm(x_ref[0], g_ref[...], mod_ref[0, 0]).astype(BF16)
    z = _dot(h, win_ref[...])
    w = NA_WIDTH
    q_ref[0] = (z[:, :w] * NA_SCALE).astype(BF16)
    k_ref[0] = z[:, w:2 * w].astype(BF16)
    v_ref[0] = z[:, 2 * w:3 * w].astype(BF16)
    gate_ref[0] = z[:, 3 * w:]


def _odd_in(xc, g, mod, win):
    b, s, d = xc.shape
    tb = TOKEN_BLOCK
    const = lambda bb, t: (0, 0)
    tok = lambda width: pl.BlockSpec((1, tb, width), lambda bb, t: (bb, t, 0))
    return pl.pallas_call(
        _odd_in_kernel,
        grid=(b, s // tb),
        in_specs=[tok(d),
                  pl.BlockSpec((1, d), const),
                  pl.BlockSpec((1, 1, 3, d), lambda bb, t: (bb, _seg(t), 0, 0)),
                  pl.BlockSpec((d, 4 * NA_WIDTH), const)],
        out_specs=[tok(NA_WIDTH)] * 4,
        out_shape=[jax.ShapeDtypeStruct((b, s, NA_WIDTH), BF16)] * 3
        + [jax.ShapeDtypeStruct((b, s, NA_WIDTH), F32)],
        compiler_params=_cparams(("parallel", "parallel")),
        name="odd_in",
    )(xc, g, mod, win)


def _na_tables(rpb, n_rows):
    h = rpb.shape[0]
    nblk = n_rows // NA_Q_ROWS
    win_rows = NA_WIN_BLOCKS * NA_Q_ROWS
    col = jnp.arange(GRID_W, dtype=jnp.int32)
    cs = jnp.clip(col - NA_KW // 2, 0, GRID_W - NA_KW)
    colmask = (col[None, :] >= cs[:, None]) & (col[None, :] < cs[:, None] + NA_KW)
    padded = jnp.pad(rpb, ((0, 0), (0, 0), (GRID_W, GRID_W)))
    tile = jnp.stack([padded[:, :, GRID_W + NA_KW - 1 - c:2 * GRID_W + NA_KW - 1 - c]
                      for c in range(GRID_W)], axis=2)
    tile = jnp.where(colmask[None, None], tile, NEG_INF)
    neg = jnp.full((h, GRID_W, GRID_W), NEG_INF, F32)
    tabs = []
    for i in (0, 1, nblk - 1):
        ws = NA_Q_ROWS * min(max(i - 1, 0), nblk - NA_WIN_BLOCKS)
        qrows = []
        for a in range(NA_Q_ROWS):
            r = NA_Q_ROWS * i + a
            rs = min(max(r - NA_KH // 2, 0), n_rows - NA_KH)
            pieces = [tile[:, ws + kr - r + NA_KH - 1] if rs <= ws + kr < rs + NA_KH else neg
                      for kr in range(win_rows)]
            qrows.append(jnp.concatenate(pieces, axis=-1))
        tab = jnp.concatenate(qrows, axis=-2)
        tabs.append(tab.reshape(h // 2, 2, NA_Q_ROWS * GRID_W, win_rows * GRID_W))
    return jnp.stack(tabs)


def _na_kernel(q_ref, kc_ref, k0_ref, k1_ref, k2_ref, vc_ref, v0_ref, v1_ref, v2_ref,
               tab_ref, o_ref):
    tb = TOKEN_BLOCK
    lane = lax.broadcasted_iota(jnp.int32, (tb, LANES), 1)

    def attend(with_window):
        krefs = [kc_ref] + ([k0_ref, k1_ref, k2_ref] if with_window else [])
        vrefs = [vc_ref] + ([v0_ref, v1_ref, v2_ref] if with_window else [])
        tab = tab_ref[0, 0].reshape(2 * tb, NA_WIN_BLOCKS * tb) if with_window else None
        for bi in range(q_ref.shape[0]):
            q32 = q_ref[bi].astype(F32)
            q = jnp.concatenate([jnp.where(lane < NA_DH, q32, 0.0), jnp.where(lane >= NA_DH, q32, 0.0)],
                                axis=0).astype(BF16)
            ss = []
            for j, kr in enumerate(krefs):
                s = _dot_nt(q, kr[bi])
                if j > 0:
                    s = s + tab[:, (j - 1) * tb:j * tb]
                ss.append(s)
            m = jnp.max(functools.reduce(jnp.maximum, ss), axis=-1, keepdims=True)
            ps = [jnp.exp(s - m) for s in ss]
            l = jnp.sum(functools.reduce(jnp.add, ps), axis=-1, keepdims=True)
            o = functools.reduce(jnp.add, [_dot(p.astype(BF16), vr[bi]) for p, vr in zip(ps, vrefs)]) / l
            o_ref[bi] = jnp.where(lane < NA_DH, o[:tb], o[tb:])

    @pl.when(pl.program_id(1) == 0)
    def _():
        attend(False)

    @pl.when(pl.program_id(1) > 0)
    def _():
        attend(True)


def _na(q, k, v, tabs):
    b, s, _ = q.shape
    tb = TOKEN_BLOCK
    nblk = s // tb
    nlat = nblk - 1
    nb = NA_BATCH_PER_STEP if b % NA_BATCH_PER_STEP == 0 else 1

    def win(j):
        return lambda hp, t, bb: (bb, 1 + jnp.clip(t - 2, 0, nlat - NA_WIN_BLOCKS) + j, hp)

    def variant(hp, t, bb):
        return (jnp.where(t <= 1, 0, jnp.where(t == nlat, 2, 1)), hp, 0, 0, 0)

    cur = pl.BlockSpec((nb, tb, LANES), lambda hp, t, bb: (bb, t, hp))
    ctx = pl.BlockSpec((nb, tb, LANES), lambda hp, t, bb: (bb, 0, hp))
    wins = [pl.BlockSpec((nb, tb, LANES), win(j)) for j in range(NA_WIN_BLOCKS)]
    return pl.pallas_call(
        _na_kernel,
        grid=(NA_HEADS // 2, nblk, b // nb),
        in_specs=[cur, ctx] + wins + [ctx] + wins
        + [pl.BlockSpec((1, 1, 2, tb, NA_WIN_BLOCKS * tb), variant)],
        out_specs=cur,
        out_shape=jax.ShapeDtypeStruct((b, s, NA_WIDTH), F32),
        compiler_params=_cparams(("parallel", "parallel", "parallel")),
        name="na_attn",
    )(q, k, k, k, k, v, v, v, v, tabs)


def _odd_out_kernel(x_ref, mod_ref, o_ref, gate_ref, wout_ref, out_ref):
    br = (o_ref[0] * jax.nn.silu(gate_ref[0])).astype(BF16)
    out_ref[0] = x_ref[0] + mod_ref[0, 0][2:3] * _dot(br, wout_ref[...])


def _odd_out_final_kernel(x_ref, mod_ref, o_ref, gate_ref, wout_ref, fg_ref, out_ref):
    br = (o_ref[0] * jax.nn.silu(gate_ref[0])).astype(BF16)
    xn = x_ref[0] + mod_ref[0, 0][2:3] * _dot(br, wout_ref[...])
    out_ref[0] = _rms(xn, fg_ref[...])


def _odd_out(xc, mod, o, gate, wout, final_g=None):
    b, s, d = xc.shape
    tb = TOKEN_BLOCK
    const = lambda bb, t: (0, 0)
    final = final_g is not None
    off = 1 if final else 0
    tok = lambda width: pl.BlockSpec((1, tb, width), lambda bb, t: (bb, t + off, 0))
    in_specs = [tok(d),
                pl.BlockSpec((1, 1, 3, d), lambda bb, t: (bb, _seg(t + off), 0, 0)),
                tok(NA_WIDTH), tok(NA_WIDTH),
                pl.BlockSpec((NA_WIDTH, d), const)]
    args = [xc, mod, o, gate, wout]
    if final:
        in_specs.append(pl.BlockSpec((1, d), const))
        args.append(final_g)
    return pl.pallas_call(
        _odd_out_final_kernel if final else _odd_out_kernel,
        grid=(b, s // tb - off),
        in_specs=in_specs,
        out_specs=pl.BlockSpec((1, tb, d), lambda bb, t: (bb, t, 0)),
        out_shape=jax.ShapeDtypeStruct((b, s - off * tb, d), F32),
        compiler_params=_cparams(("parallel", "parallel")),
        name="odd_out_final" if final else "odd_out",
    )(*args)


def _rope_rot(w):
    w4 = w.reshape(w.shape[:-1] + (2, 2, MLA_ROPE // 4))
    return jnp.stack([-w4[..., 1, :], w4[..., 0, :]], axis=-2).reshape(w.shape)


def _rope_tables(n_lat, n_ctx):
    t = jnp.arange(n_lat, dtype=jnp.int32)
    row = (t // GRID_W).astype(F32)
    col = (t % GRID_W).astype(F32)
    nf = MLA_ROPE // 4
    freqs = ROPE_BASE ** (-jnp.arange(nf, dtype=F32) / nf)
    ar = row[:, None] * freqs
    ac = col[:, None] * freqs
    cos = jnp.concatenate([jnp.cos(ar), jnp.cos(ar), jnp.cos(ac), jnp.cos(ac)], axis=-1)
    sin = jnp.concatenate([jnp.sin(ar), jnp.sin(ar), jnp.sin(ac), jnp.sin(ac)], axis=-1)
    pad = LANES - MLA_QK
    cos = jnp.concatenate([jnp.ones((n_lat, MLA_NOPE), F32), cos, jnp.ones((n_lat, pad), F32)], axis=-1)
    sin = jnp.concatenate([jnp.zeros((n_lat, MLA_NOPE), F32), sin, jnp.zeros((n_lat, pad), F32)], axis=-1)
    cos = jnp.concatenate([jnp.ones((n_ctx, LANES), F32), cos], axis=0)
    sin = jnp.concatenate([jnp.zeros((n_ctx, LANES), F32), sin], axis=0)
    return cos, sin


def _even_weights(w_in, q_norm, kv_norm, w_uq, w_ukv, d_skip, w_glu, b_glu, w_out):
    d = w_in.shape[0]
    s1 = MLA_Q_RANK
    s2 = s1 + MLA_KV_RANK
    s3 = s2 + MLA_ROPE
    kr = w_in[:, s2:s3]
    z_lo = jnp.zeros((d, MLA_NOPE), F32)
    z_hi = jnp.zeros((d, LANES - MLA_QK), F32)
    win = jnp.concatenate([w_in[:, :s2], z_lo, kr, z_hi, z_lo, _rope_rot(kr), z_hi, w_in[:, s3:]], axis=1)
    uq = w_uq.reshape(MLA_Q_RANK, MLA_HEADS, MLA_QK)
    zq_lo = jnp.zeros((MLA_Q_RANK, MLA_HEADS, MLA_NOPE), F32)
    zq_hi = jnp.zeros((MLA_Q_RANK, MLA_HEADS, LANES - MLA_QK), F32)
    wuqa = jnp.concatenate([uq, zq_hi], axis=-1).reshape(MLA_Q_RANK, MLA_HEADS * LANES)
    wuqb = jnp.concatenate([zq_lo, _rope_rot(uq[..., MLA_NOPE:]), zq_hi], axis=-1).reshape(MLA_Q_RANK, MLA_HEADS * LANES)
    ukv = w_ukv.reshape(MLA_KV_RANK, MLA_HEADS, MLA_NOPE + MLA_V)
    zk = jnp.zeros((MLA_KV_RANK, MLA_HEADS, LANES - MLA_NOPE), F32)
    wk = jnp.concatenate([ukv[..., :MLA_NOPE], zk], axis=-1).reshape(MLA_KV_RANK, MLA_HEADS * LANES)
    wv = ukv[..., MLA_NOPE:].reshape(MLA_KV_RANK, MLA_WIDTH)
    return dict(win=win.astype(BF16), qn=q_norm[None], kvn=kv_norm[None],
                wuqa=wuqa.astype(BF16), wuqb=wuqb.astype(BF16),
                wukv=jnp.concatenate([wk, wv], axis=1).astype(BF16),
                d=d_skip[None], wglu=w_glu.astype(BF16), bglu=b_glu[None], wout=w_out.astype(BF16))


def kernel(x, c, ctx, c_ctx, ada_w, ada_b, norm_g, final_g, ev_w_in, ev_q_norm, ev_kv_norm, ev_w_uq, ev_w_ukv, s5_lam_re, s5_lam_im, s5_log_dt, s5_b_re, s5_b_im, s5_c_re, s5_c_im, s5_d, s5_w_glu, s5_b_glu, ev_w_out, na_w_in, na_rpb, na_w_out):
    b, n_lat, d = x.shape
    n_ctx = ctx.shape[1]
    depth = ada_w.shape[0]
    assert n_ctx == TOKEN_BLOCK and n_lat % TOKEN_BLOCK == 0
    n_rows = n_lat // GRID_W
    assert n_rows // NA_Q_ROWS >= NA_WIN_BLOCKS + 1
    assert depth % 2 == 0, "the final norm is fused into an odd (last) layer"

    xc = jnp.concatenate([ctx, x], axis=1)
    mods = _adaln(c, c_ctx, ada_w, ada_b)
    cos_t, sin_t = _rope_tables(n_lat, n_ctx)

    for l in range(depth):
        i = l // 2
        g = norm_g[l][None]
        mod = mods[l]
        if l % 2 == 0:
            w = _even_weights(ev_w_in[i], ev_q_norm[i], ev_kv_norm[i], ev_w_uq[i], ev_w_ukv[i],
                              s5_d[i], s5_w_glu[i], s5_b_glu[i], ev_w_out[i])
            s5p = _s5_prepare(s5_lam_re[i], s5_lam_im[i], s5_log_dt[i], s5_b_re[i], s5_b_im[i],
                              s5_c_re[i], s5_c_im[i])
            q, k, v, gm, u, gs, up = _even_in(xc, g, mod, cos_t, sin_t, w)
            o_ctx, o_lat = _mla(q, k, v)
            yf = _s5_scan(up, *s5p[0], reverse=False)
            yb = _s5_scan(up, *s5p[1], reverse=True)
            xc = _even_out(xc, mod, o_ctx, o_lat, gm, u, yf, yb, gs, w)
        else:
            q, k, v, gate = _odd_in(xc, g, mod, na_w_in[i].astype(BF16))
            o = _na(q, k, v, _na_tables(na_rpb[i], n_rows))
            last = l == depth - 1
            xc = _odd_out(xc, mod, o, gate, na_w_out[i].astype(BF16), final_g[None] if last else None)
    return xc
```
